```python
import math
import jax, jax.numpy as jnp
from jax import lax
import numpy as np

D_MODEL = 1024
BATCH = 8
SEQ = 8192
DEPTH = 4
DEC_BATCH = 32
DEC_SEQ = 2048
PAST_LEN = 128

GRID_W = 64
EPS = 1e-6
LB_FLOOR = 1e-30

M_HEADS = 8
M_HEAD_DIM = 64
M_INNER = M_HEADS * M_HEAD_DIM
M_GROUPS = 2
M_HPG = M_HEADS // M_GROUPS
M_STATE = 64
M_CONV = 4
M_CHUNK = 128
M_CONV_DIM = M_INNER + 2 * M_GROUPS * M_STATE

H_HEADS = 4
H_KEY = 128
H_VAL = 128
H_QK_WIDTH = H_HEADS * H_KEY
H_V_WIDTH = H_HEADS * H_VAL
H_CHUNK = 16

A_Q_HEADS = 8
A_KV_HEADS = 2
A_Q_PER_KV = A_Q_HEADS // A_KV_HEADS
A_HEAD_DIM = 64
A_WIDTH = A_Q_HEADS * A_HEAD_DIM
A_Q_BLOCK = 128
ROPE_AXIS_DIM = A_HEAD_DIM // 2
ROPE_BASE = 10000.0

N_BRANCH = 3
BRANCH_WIDTH = 512

D_FF = 2816
FFN_CONV = 3

IN_SIZES = (M_INNER, M_CONV_DIM, M_HEADS, M_HEADS,
            H_QK_WIDTH, H_QK_WIDTH, H_QK_WIDTH, H_V_WIDTH, H_V_WIDTH,
            A_WIDTH, A_KV_HEADS * A_HEAD_DIM, A_KV_HEADS * A_HEAD_DIM,
            N_BRANCH * D_MODEL)
D_IN = sum(IN_SIZES)

kernel_name = "hybrid_ssd_hgrn2_axial_gqa_encoder"


def _rmsnorm(x, w):
    xf = x.astype(jnp.float32)
    y = xf * lax.rsqrt(jnp.mean(xf * xf, axis=-1, keepdims=True) + EPS)
    return (y * w.astype(jnp.float32)).astype(x.dtype)


def _dwconv(x, w, b, pad_left, pad_right):
    c = x.shape[-1]
    y = lax.conv_general_dilated(x, w[:, None, :].astype(x.dtype), window_strides=(1,),
                                 padding=[(pad_left, pad_right)],
                                 dimension_numbers=('NWC', 'WIO', 'NWC'),
                                 feature_group_count=c)
    return y + b.astype(x.dtype)


def _split_cols(proj):
    offs, acc = [], 0
    for s in IN_SIZES[:-1]:
        acc += s
        offs.append(acc)
    return jnp.split(proj, offs, axis=-1)


def _ssd_chunked(xs, dt, A, Bm, Cm):
    b, L, G, R, P = xs.shape
    N = Bm.shape[-1]
    nc = L // M_CHUNK
    f32 = jnp.float32
    xs = xs.astype(f32).reshape(b, nc, M_CHUNK, G, R, P)
    Bm = Bm.astype(f32).reshape(b, nc, M_CHUNK, G, N)
    Cm = Cm.astype(f32).reshape(b, nc, M_CHUNK, G, N)
    dt = dt.reshape(b, nc, M_CHUNK, G, R)
    a_cum = jnp.cumsum(jnp.moveaxis(dt * A, 2, -1), axis=-1)
    causal = jnp.tril(jnp.ones((M_CHUNK, M_CHUNK), dtype=bool))
    seg = a_cum[..., :, None] - a_cum[..., None, :]
    decay_in = jnp.where(causal, jnp.exp(jnp.minimum(seg, 0.0)), 0.0)
    xdt = xs * dt[..., None]
    cb = jnp.einsum('bclgn,bcsgn->bcgls', Cm, Bm)
    y_diag = jnp.einsum('bcgls,bcgrls,bcsgrp->bclgrp', cb, decay_in, xdt)
    decay_out = jnp.exp(a_cum[..., -1:] - a_cum)
    states = jnp.einsum('bclgn,bcgrl,bclgrp->bcgrpn', Bm, decay_out, xdt)
    chunk_decay = jnp.exp(a_cum[..., -1])

    def step(S, inp):
        st, dec = inp
        return S * dec[..., None, None] + st, S

    S0 = jnp.zeros((b, G, R, P, N), f32)
    _, S_prev = lax.scan(step, S0, (jnp.moveaxis(states, 1, 0), jnp.moveaxis(chunk_decay, 1, 0)))
    S_prev = jnp.moveaxis(S_prev, 0, 1)
    y_off = jnp.einsum('bclgn,bcgrpn,bcgrl->bclgrp', Cm, S_prev, jnp.exp(a_cum))
    return (y_diag + y_off).reshape(b, L, G, R, P)


def _mamba_branch(z, xbc, dt_raw_f, dt_raw_b, conv_w, conv_b, dt_bias, a_log, d_skip, norm_w):
    b, L, _ = z.shape
    f32 = jnp.float32
    xbc = jax.nn.silu(_dwconv(xbc, conv_w, conv_b, M_CONV // 2, M_CONV - 1 - M_CONV // 2))
    xs = xbc[..., :M_INNER].reshape(b, L, M_GROUPS, M_HPG, M_HEAD_DIM)
    Bm = xbc[..., M_INNER:M_INNER + M_GROUPS * M_STATE].reshape(b, L, M_GROUPS, M_STATE)
    Cm = xbc[..., M_INNER + M_GROUPS * M_STATE:].reshape(b, L, M_GROUPS, M_STATE)

    def dir_params(raw, d):
        dt = jax.nn.softplus(raw.astype(f32) + dt_bias[d].astype(f32)).reshape(b, L, M_GROUPS, M_HPG)
        A = -jnp.exp(a_log[d].astype(f32)).reshape(M_GROUPS, M_HPG)
        return dt, A

    dt_f, A_f = dir_params(dt_raw_f, 0)
    dt_b, A_b = dir_params(dt_raw_b, 1)
    flip = lambda t: jnp.flip(t, axis=1)
    y_f = _ssd_chunked(xs, dt_f, A_f, Bm, Cm)
    y_b = flip(_ssd_chunked(flip(xs), flip(dt_b), A_b, flip(Bm), flip(Cm)))
    y = y_f + y_b + xs.astype(f32) * d_skip.astype(f32).reshape(M_GROUPS, M_HPG, 1)
    y = y.reshape(b, L, M_INNER) * jax.nn.silu(z.astype(f32))
    y = y.reshape(b, L, M_GROUPS, M_INNER // M_GROUPS)
    y = y * lax.rsqrt(jnp.mean(y * y, axis=-1, keepdims=True) + EPS)
    return (y.reshape(b, L, M_INNER) * norm_w.astype(f32)).astype(z.dtype)


def _hgrn_chunked(q, k, v, logf):
    b, L, H, K = q.shape
    V = v.shape[-1]
    nc = L // H_CHUNK

    def chunk(t):
        return jnp.moveaxis(t.reshape(b, nc, H_CHUNK, H, t.shape[-1]), 3, 2)

    q, k, v, logf = chunk(q), chunk(k), chunk(v), chunk(logf)
    bcum = jnp.cumsum(logf, axis=3)
    diff = bcum[:, :, :, :, None, :] - bcum[:, :, :, None, :, :]
    causal = jnp.tril(jnp.ones((H_CHUNK, H_CHUNK), dtype=bool))[:, :, None]
    decay = jnp.where(causal, jnp.exp(jnp.minimum(diff, 0.0)), 0.0)
    att = jnp.einsum('bchlk,bchlsk,bchsk->bchls', q, decay, k)
    o_intra = jnp.einsum('bchls,bchsv->bchlv', att, v)
    chunk_states = jnp.einsum('bchsk,bchsv->bchkv', k * jnp.exp(bcum[:, :, :, -1:] - bcum), v)
    chunk_decay = jnp.exp(bcum[:, :, :, -1])

    def step(S, inp):
        st, dec = inp
        return S * dec[..., None] + st, S

    S0 = jnp.zeros((b, H, K, V), jnp.float32)
    _, S_prev = lax.scan(step, S0, (jnp.moveaxis(chunk_states, 1, 0), jnp.moveaxis(chunk_decay, 1, 0)))
    S_prev = jnp.moveaxis(S_prev, 0, 1)
    o_inter = jnp.einsum('bchlk,bchkv->bchlv', q * jnp.exp(bcum), S_prev)
    return jnp.moveaxis(o_intra + o_inter, 2, 3).reshape(b, L, H, V)


def _hgrn_branch(q, f_raw_f, f_raw_b, i, g, lb, norm_w):
    b, L, _ = q.shape
    f32 = jnp.float32
    qh = (q.astype(f32) * H_KEY ** -0.5).reshape(b, L, H_HEADS, H_KEY)
    vh = i.astype(f32).reshape(b, L, H_HEADS, H_VAL)

    def gates(raw, lbd):
        lbd = lbd.astype(f32)
        rawf = raw.astype(f32)
        logf = jnp.logaddexp(jnp.log(jnp.maximum(lbd, LB_FLOOR)), jnp.log1p(-lbd) + jax.nn.log_sigmoid(rawf))
        key = (1.0 - lbd) * jax.nn.sigmoid(-rawf)
        return key.reshape(b, L, H_HEADS, H_KEY), logf.reshape(b, L, H_HEADS, H_KEY)

    k_f, logf_f = gates(f_raw_f, lb[0])
    k_b, logf_b = gates(f_raw_b, lb[1])
    flip = lambda t: jnp.flip(t, axis=1)
    o = _hgrn_chunked(qh, k_f, vh, logf_f) + flip(_hgrn_chunked(flip(qh), flip(k_b), flip(vh), flip(logf_b)))
    o = o * lax.rsqrt(jnp.mean(o * o, axis=-1, keepdims=True) + EPS) * norm_w.astype(f32)
    o = o * jax.nn.silu(g.astype(f32).reshape(b, L, H_HEADS, H_VAL))
    return o.reshape(b, L, H_V_WIDTH).astype(q.dtype)


def _axial_rope(n):
    f32 = jnp.float32
    rows = n // GRID_W
    row = jnp.repeat(jnp.arange(rows, dtype=f32), GRID_W)
    col = jnp.tile(jnp.arange(GRID_W, dtype=f32), rows)
    inv = ROPE_BASE ** (-jnp.arange(0, ROPE_AXIS_DIM, 2, dtype=f32) / ROPE_AXIS_DIM)
    ang_r = row[:, None] * inv
    ang_c = col[:, None] * inv
    return (jnp.cos(ang_r), jnp.sin(ang_r), jnp.cos(ang_c), jnp.sin(ang_c))


def _rotate(x, cos, sin):
    x1, x2 = jnp.split(x, 2, axis=-1)
    c = cos[None, :, None, :]
    s = sin[None, :, None, :]
    return jnp.concatenate([x1 * c - x2 * s, x2 * c + x1 * s], axis=-1)


def _apply_axial_rope(x, rope):
    cr, sr, cc, sc = rope
    return jnp.concatenate([_rotate(x[..., :ROPE_AXIS_DIM], cr, sr),
                            _rotate(x[..., ROPE_AXIS_DIM:], cc, sc)], axis=-1)


def _attn_branch(q, k, v, qn_w, kn_w, rope):
    b, L, _ = q.shape
    f32 = jnp.float32
    q = q.astype(f32).reshape(b, L, A_Q_HEADS, A_HEAD_DIM)
    k = k.astype(f32).reshape(b, L, A_KV_HEADS, A_HEAD_DIM)
    v = v.astype(f32).reshape(b, L, A_KV_HEADS, A_HEAD_DIM)
    q = _apply_axial_rope(_rmsnorm(q, qn_w), rope) * A_HEAD_DIM ** -0.5
    k = _apply_axial_rope(_rmsnorm(k, kn_w), rope)
    nb = L // A_Q_BLOCK
    qb = jnp.moveaxis(q.reshape(b, nb, A_Q_BLOCK, A_KV_HEADS, A_Q_PER_KV, A_HEAD_DIM), 1, 0)

    def block(qi):
        s = jnp.einsum('bqgrd,bkgd->bgrqk', qi, k)
        p = jax.nn.softmax(s, axis=-1)
        return jnp.einsum('bgrqk,bkgd->bqgrd', p, v)

    o = lax.map(block, qb)
    return jnp.moveaxis(o, 0, 1).reshape(b, L, A_WIDTH).astype(q.dtype)


def setup_inputs(seed: int = 0) -> dict:
    key = jax.random.key(seed)
    ks = jax.random.split(key, 24)
    f32 = jnp.float32

    def nrm(k, shape, scale):
        return jax.random.normal(k, shape, f32) * scale

    def gain(k, shape):
        return 1.0 + 0.02 * jax.random.normal(k, shape, f32)

    dt = jnp.exp(jax.random.uniform(ks[6], (DEPTH, 2, M_HEADS), f32, math.log(1e-3), math.log(1e-1)))
    return {
        "x_prompt": jax.random.normal(ks[0], (BATCH, SEQ, D_MODEL), f32),
        "x_sample": jax.random.normal(ks[1], (DEC_BATCH, DEC_SEQ, D_MODEL), f32),
        "norm_mix_pre": gain(ks[2], (DEPTH, D_MODEL)),
        "w_in": nrm(ks[3], (DEPTH, D_MODEL, D_IN), D_MODEL ** -0.5),
        "m_conv_w": nrm(ks[4], (DEPTH, M_CONV, M_CONV_DIM), M_CONV ** -0.5),
        "m_conv_b": nrm(ks[5], (DEPTH, M_CONV_DIM), 0.02),
        "m_dt_bias": dt + jnp.log(-jnp.expm1(-dt)),
        "m_a_log": jnp.log(jax.random.uniform(ks[7], (DEPTH, 2, M_HEADS), f32, 1.0, 16.0)),
        "m_d": gain(ks[8], (DEPTH, M_HEADS)),
        "m_norm_w": gain(ks[9], (DEPTH, M_INNER)),
        "h_lb_logits": nrm(ks[10], (DEPTH, 2, H_QK_WIDTH), 0.5),
        "h_norm_w": gain(ks[11], (DEPTH, H_VAL)),
        "a_q_norm": gain(ks[12], (DEPTH, A_HEAD_DIM)),
        "a_k_norm": gain(ks[13], (DEPTH, A_HEAD_DIM)),
        "w_branch": nrm(ks[14], (DEPTH, N_BRANCH, BRANCH_WIDTH, D_MODEL), BRANCH_WIDTH ** -0.5),
        "w_out": nrm(ks[15], (DEPTH, D_MODEL, D_MODEL), D_MODEL ** -0.5),
        "norm_mix_post": gain(ks[16], (DEPTH, D_MODEL)),
        "norm_ffn_pre": gain(ks[17], (DEPTH, D_MODEL)),
        "f_w_up": nrm(ks[18], (DEPTH, D_MODEL, 2 * D_FF), D_MODEL ** -0.5),
        "f_conv_w": nrm(ks[19], (DEPTH, FFN_CONV, 2 * D_FF), FFN_CONV ** -0.5),
        "f_conv_b": nrm(ks[20], (DEPTH, 2 * D_FF), 0.02),
        "f_w_down": nrm(ks[21], (DEPTH, D_FF, D_MODEL), D_FF ** -0.5),
        "norm_ffn_post": gain(ks[22], (DEPTH, D_MODEL)),
    }


def reference(x_prompt, x_sample, norm_mix_pre, w_in, m_conv_w, m_conv_b, m_dt_bias, m_a_log, m_d,
              m_norm_w, h_lb_logits, h_norm_w, a_q_norm, a_k_norm, w_branch, w_out, norm_mix_post,
              norm_ffn_pre, f_w_up, f_conv_w, f_conv_b, f_w_down, norm_ffn_post):
    sm = jax.nn.softmax(h_lb_logits.astype(jnp.float32), axis=0)
    lower_bounds = jnp.cumsum(sm, axis=0) - sm[0:1]

    def run(x):
        dtype = x.dtype
        b, n, _ = x.shape
        rope = _axial_rope(n)
        for l in range(DEPTH):
            h = _rmsnorm(x, norm_mix_pre[l])
            (m_z, m_xbc, m_dtf, m_dtb, h_q, h_ff, h_fb, h_i, h_g,
             a_q, a_k, a_v, gate_raw) = _split_cols(h @ w_in[l])
            y_m = _mamba_branch(m_z, m_xbc, m_dtf, m_dtb, m_conv_w[l], m_conv_b[l], m_dt_bias[l],
                                m_a_log[l], m_d[l], m_norm_w[l])
            y_h = _hgrn_branch(h_q, h_ff, h_fb, h_i, h_g, lower_bounds[l], h_norm_w[l])
            y_a = _attn_branch(a_q, a_k, a_v, a_q_norm[l], a_k_norm[l], rope)
            branches = jnp.stack([y_m, y_h, y_a], axis=2)
            proj_b = jnp.einsum('bnic,icd->bnid', branches, w_branch[l])
            gates = jax.nn.sigmoid(gate_raw.astype(jnp.float32)).reshape(b, n, N_BRANCH, D_MODEL)
            mixed = jnp.sum(gates.astype(dtype) * proj_b, axis=2) @ w_out[l]
            x = x + _rmsnorm(mixed, norm_mix_post[l])
            h = _rmsnorm(x, norm_ffn_pre[l])
            u = _dwconv(h @ f_w_up[l], f_conv_w[l], f_conv_b[l], FFN_CONV // 2, FFN_CONV // 2)
            u_gate, u_up = jnp.split(u, 2, axis=-1)
            ff = (jax.nn.silu(u_gate) * u_up) @ f_w_down[l]
            x = x + _rmsnorm(ff, norm_ffn_post[l])
        return x

    y_prompt = run(x_prompt)
    y_sample = run(x_sample)
    return (y_prompt, y_sample)
```

```python
import functools

import jax
import jax.numpy as jnp
from jax import lax
from jax.experimental import pallas as pl
from jax.experimental.pallas import tpu as pltpu

F32 = jnp.float32
BF16 = jnp.bfloat16

D_MODEL = 1024
DEPTH = 4
EPS = 1e-6
LB_FLOOR = 1e-30
GRID_W = 64
ROPE_BASE = 10000.0

M_HEADS = 8
M_INNER = 512
M_STATE = 64
M_CONV = 4
M_XBC = 768
SSD_CHUNK = 128

H_HEADS = 4
H_KEY = 128
H_WIDTH = 512
HGRN_CHUNK = 128
HGRN_DIAG = 16

A_HEADS = 8
A_KV = 2
A_HD = 64
A_GROUP = A_HEADS // A_KV

D_FF = 2816
FFN_COLS = 256
FFN_HALO = 8
XBC_HALO = 16

LANE = 128
VMEM_LIMIT = 56 * 1024 * 1024

OFF_Z = 0
OFF_XBC = OFF_Z + M_INNER
OFF_DT = OFF_XBC + M_XBC
OFF_H = OFF_DT + LANE
OFF_Q = OFF_H + 5 * H_WIDTH
OFF_K = OFF_Q + A_HEADS * LANE
OFF_V = OFF_K + A_KV * LANE
OFF_G = OFF_V + A_KV * LANE
N_PACKED = OFF_G + 3 * D_MODEL


def _rms(x, w):
    return x * lax.rsqrt(jnp.mean(x * x, axis=-1, keepdims=True) + EPS) * w


def _sigmoid(x):
    return 1.0 / (1.0 + jnp.exp(-x))


def _dot(a, b):
    return jnp.dot(a, b, preferred_element_type=F32)


def _dot_nt(a, b):
    return lax.dot_general(a, b, (((1,), (1,)), ((), ())), preferred_element_type=F32)


def _split_dot(tri, x):
    hi = x.astype(BF16)
    lo = (x - hi.astype(F32)).astype(BF16)
    return _dot(tri, hi) + _dot(tri, lo)


def _params(**kw):
    return pltpu.CompilerParams(vmem_limit_bytes=VMEM_LIMIT, **kw)


def _inproj_kernel(x_ref, nw_ref, w_ref, cos_ref, sin_ref, qw_ref, kw_ref,
                   oz, oxbc, odt, oh, oq, ok, ov, og):
    h = _rms(x_ref[...], nw_ref[...]).astype(BF16)

    def mm(c0, n):
        return _dot(h, w_ref[:, c0:c0 + n])

    def plain(o_ref, c0, width, step):
        for j in range(0, width, step):
            o_ref[:, j:j + step] = mm(c0 + j, step).astype(o_ref.dtype)

    plain(oz, OFF_Z, M_INNER, 512)
    plain(oxbc, OFF_XBC, M_XBC, 256)
    odt[...] = mm(OFF_DT, LANE)
    plain(oh, OFF_H, 5 * H_WIDTH, 512)

    cos = cos_ref[...]
    sin = sin_ref[...]
    lane = lax.broadcasted_iota(jnp.int32, cos.shape, 1)
    first_half = (lane & 16) == 0

    def norm_rope(y, w):
        ss = jnp.sum(y * y, axis=-1, keepdims=True) * (1.0 / A_HD)
        yn = y * lax.rsqrt(ss + EPS) * w
        partner = jnp.where(first_half, pltpu.roll(yn, LANE - 16, 1), pltpu.roll(yn, 16, 1))
        return yn * cos + partner * sin

    qw = qw_ref[...]
    kw = kw_ref[...]
    for hd in range(A_HEADS):
        oq[:, hd * LANE:(hd + 1) * LANE] = norm_rope(mm(OFF_Q + hd * LANE, LANE), qw).astype(BF16)
    for hd in range(A_KV):
        ok[:, hd * LANE:(hd + 1) * LANE] = norm_rope(mm(OFF_K + hd * LANE, LANE), kw).astype(BF16)
    plain(ov, OFF_V, A_KV * LANE, 256)
    plain(og, OFF_G, 3 * D_MODEL, 512)


def _inproj(x, nw, w, cos, sin, qw, kw, seq, tm):
    t = x.shape[0]
    tiles_per_seq = seq // tm
    row = lambda i: (i, 0)
    const = lambda i: (0, 0)
    pos = lambda i: (i % tiles_per_seq, 0)
    widths = (M_INNER, M_XBC, LANE, 5 * H_WIDTH, A_HEADS * LANE, A_KV * LANE, A_KV * LANE, 3 * D_MODEL)
    dtypes = (BF16, BF16, F32, BF16, BF16, BF16, BF16, BF16)
    return pl.pallas_call(
        _inproj_kernel,
        grid=(t // tm,),
        in_specs=[
            pl.BlockSpec((tm, D_MODEL), row),
            pl.BlockSpec((1, D_MODEL), const),
            pl.BlockSpec((D_MODEL, N_PACKED), const, pipeline_mode=pl.Buffered(1)),
            pl.BlockSpec((tm, LANE), pos),
            pl.BlockSpec((tm, LANE), pos),
            pl.BlockSpec((1, LANE), const),
            pl.BlockSpec((1, LANE), const),
        ],
        out_specs=[pl.BlockSpec((tm, wd), row) for wd in widths],
        out_shape=[jax.ShapeDtypeStruct((t, wd), dt) for wd, dt in zip(widths, dtypes)],
        compiler_params=_params(dimension_semantics=("parallel",)),
        name="inproj",
    )(x, nw, w, cos, sin, qw, kw)


def _ssd_kernel(rev, nc, *refs):
    if rev:
        (xp_ref, xm_ref, xn_ref, dt_ref, cw_ref, cb_ref, dtb_ref, a_ref, tri_ref,
         yin_ref, z_ref, nw_ref, o_ref, s_ref) = refs
    else:
        (xp_ref, xm_ref, xn_ref, dt_ref, cw_ref, cb_ref, dtb_ref, a_ref, tri_ref,
         dsk_ref, o_ref, s_ref) = refs
    c = pl.program_id(1)
    cp = nc - 1 - c if rev else c
    q = xm_ref.shape[0]

    @pl.when(c == 0)
    def _():
        s_ref[...] = jnp.zeros_like(s_ref)

    xp = jnp.where(cp > 0, xp_ref[...].astype(F32), 0.0)
    xn = jnp.where(cp < nc - 1, xn_ref[...].astype(F32), 0.0)
    xe = jnp.concatenate([xp, xm_ref[...].astype(F32), xn], axis=0)
    cw = cw_ref[...]
    y = cb_ref[...]
    for k in range(M_CONV):
        start = XBC_HALO - M_CONV // 2 + k
        y = y + xe[start:start + q] * cw[k:k + 1]
    xc = y * _sigmoid(y)
    xs = xc[:, :M_INNER]
    bm = xc[:, M_INNER:M_INNER + LANE]
    cm = xc[:, M_INNER + LANE:]

    dtr = dt_ref[...] + dtb_ref[...]
    dt = jnp.maximum(dtr, 0.0) + jnp.log1p(jnp.exp(-jnp.abs(dtr)))
    a = dt * a_ref[...]
    u = _split_dot(tri_ref[...], a)
    u_end = u[0:1] if rev else u[q - 1:q]
    w_state = dt * jnp.exp(u_end - u)
    off = jnp.exp(u)
    cd = jnp.exp(u_end)
    u_t = u.T
    dt_t = dt.T

    li = lax.broadcasted_iota(jnp.int32, (q, q), 0)
    si = lax.broadcasted_iota(jnp.int32, (q, q), 1)
    mask = (si >= li) if rev else (si <= li)
    lane_s = lax.broadcasted_iota(jnp.int32, (q, LANE), 1)
    lane_x = lax.broadcasted_iota(jnp.int32, (q, 2 * LANE), 1)
    lane_st = lax.broadcasted_iota(jnp.int32, (LANE, 2 * LANE), 1)
    lane_row = lax.broadcasted_iota(jnp.int32, (1, 2 * LANE), 1)
    base = M_HEADS if rev else 0
    hpg = M_HEADS // 2

    ys = []
    for g in range(2):
        gsel = (lane_s >= M_STATE) if g else (lane_s < M_STATE)
        cg = jnp.where(gsel, cm, 0.0)
        bg = jnp.where(gsel, bm, 0.0)
        cbm = _dot_nt(cg.astype(BF16), bm.astype(BF16))
        xg = xs[:, g * 2 * LANE:(g + 1) * 2 * LANE]
        sg = s_ref[g]
        lhs = []
        for r in range(hpg):
            ln = base + g * hpg + r
            dec = jnp.exp(jnp.where(mask, u[:, ln:ln + 1] - u_t[ln:ln + 1, :], -1e30))
            lhs.append((cbm * dec * dt_t[ln:ln + 1, :]).astype(BF16))
        for r in range(hpg):
            ln = base + g * hpg + r
            lhs.append((cg * off[:, ln:ln + 1]).astype(BF16))
        xr = [jnp.where((lane_x >> 6) == r, xg, 0.0).astype(BF16) for r in range(hpg)]
        sr = [jnp.where((lane_st >> 6) == r, sg, 0.0).astype(BF16) for r in range(hpg)]
        ys.append(_dot(jnp.concatenate(lhs, axis=1), jnp.concatenate(xr + sr, axis=0)))

        bw_t = []
        cdrow = jnp.zeros((1, 2 * LANE), F32)
        for r in range(hpg):
            ln = base + g * hpg + r
            bw_t.append((bg * w_state[:, ln:ln + 1]).T.astype(BF16))
            cdrow = jnp.where((lane_row >> 6) == r, cd[:, ln:ln + 1], cdrow)
        st = _dot(jnp.concatenate(bw_t, axis=1), jnp.concatenate(xr, axis=0))
        s_ref[g] = sg * cdrow + st

    yc = jnp.concatenate(ys, axis=1)
    if not rev:
        o_ref[...] = yc + xs * dsk_ref[...]
    else:
        yt = yin_ref[...] + yc
        zf = z_ref[...].astype(F32)
        yt = yt * (zf * _sigmoid(zf))
        nw = nw_ref[...]
        for g in range(2):
            sl = slice(g * 2 * LANE, (g + 1) * 2 * LANE)
            blk = yt[:, sl]
            ms = jnp.mean(blk * blk, axis=-1, keepdims=True)
            o_ref[:, sl] = (blk * lax.rsqrt(ms + EPS) * nw[:, sl]).astype(o_ref.dtype)


def _ssd(rev, nb, seq, xbc, dtraw, cw, cb, dtb, a_neg, tri, extra):
    q = SSD_CHUNK
    nc = seq // q
    t = nb * seq
    hb = q // XBC_HALO
    last_halo = t // XBC_HALO - 1

    def pos(b, c):
        return b * nc + (nc - 1 - c if rev else c)

    main = lambda b, c: (pos(b, c), 0)
    prev = lambda b, c: (jnp.maximum(pos(b, c) * hb - 1, 0), 0)
    nxt = lambda b, c: (jnp.minimum(pos(b, c) * hb + hb, last_halo), 0)
    const = lambda b, c: (0, 0)
    in_specs = [
        pl.BlockSpec((XBC_HALO, M_XBC), prev),
        pl.BlockSpec((q, M_XBC), main),
        pl.BlockSpec((XBC_HALO, M_XBC), nxt),
        pl.BlockSpec((q, LANE), main),
        pl.BlockSpec((M_CONV, M_XBC), const),
        pl.BlockSpec((1, M_XBC), const),
        pl.BlockSpec((1, LANE), const),
        pl.BlockSpec((1, LANE), const),
        pl.BlockSpec((q, q), const),
    ]
    args = [xbc, xbc, xbc, dtraw, cw, cb, dtb, a_neg, tri]
    if rev:
        y_in, z, nw = extra
        in_specs += [pl.BlockSpec((q, M_INNER), main), pl.BlockSpec((q, M_INNER), main),
                     pl.BlockSpec((1, M_INNER), const)]
        args += [y_in, z, nw]
        out_dtype = BF16
    else:
        (dskip,) = extra
        in_specs += [pl.BlockSpec((1, M_INNER), const)]
        args += [dskip]
        out_dtype = F32
    return pl.pallas_call(
        functools.partial(_ssd_kernel, rev, nc),
        grid=(nb, nc),
        in_specs=in_specs,
        out_specs=pl.BlockSpec((q, M_INNER), main),
        out_shape=jax.ShapeDtypeStruct((t, M_INNER), out_dtype),
        scratch_shapes=[pltpu.VMEM((2, LANE, 2 * LANE), F32)],
        compiler_params=_params(dimension_semantics=("arbitrary", "arbitrary")),
        name="ssd_rev" if rev else "ssd_fwd",
    )(*args)


def _hgrn_kernel(rev, *refs):
    if rev:
        (q_ref, f_ref, i_ref, lbf_ref, om_ref, tri_ref, yin_ref, g_ref, nw_ref,
         o_ref, st_ref, qs, ks, us, vs, acc) = refs
    else:
        (q_ref, f_ref, i_ref, lbf_ref, om_ref, tri_ref,
         o_ref, st_ref, qs, ks, us, vs, acc) = refs
    c = pl.program_id(1)
    n = q_ref.shape[0]

    @pl.when(c == 0)
    def _():
        st_ref[...] = jnp.zeros_like(st_ref)

    raw = f_ref[...].astype(F32)
    t = jnp.exp(-jnp.abs(raw))
    r = 1.0 / (1.0 + t)
    nonneg = raw >= 0.0
    sig = jnp.where(nonneg, r, t * r)
    sig_neg = jnp.where(nonneg, t * r, r)
    om = om_ref[...]
    logf = jnp.log(lbf_ref[...] + om * sig)
    kk = om * sig_neg
    u = _split_dot(tri_ref[...], logf)
    qf = q_ref[...].astype(F32) * (H_KEY ** -0.5)
    v = i_ref[...].astype(F32)
    qs[...] = qf
    ks[...] = kk
    us[...] = u
    vs[...] = v

    u_end = u[0:1] if rev else u[n - 1:n]
    qb = (qf * jnp.exp(u)).astype(BF16)
    kw = (kk * jnp.exp(u_end - u)).astype(BF16)
    cd = jnp.exp(u_end)
    vb = v.astype(BF16)

    for h in range(H_HEADS):
        sl = slice(h * LANE, (h + 1) * LANE)
        st = st_ref[h]
        acc[:, sl] = _dot_nt(qb[:, sl], st.astype(BF16))
        st_ref[h] = st * cd[:, sl] + _dot(v[:, sl].T.astype(BF16), kw[:, sl])

    m = HGRN_DIAG
    while 2 * m <= n:
        for p in range(n // (2 * m)):
            lo, mid, hi = 2 * m * p, 2 * m * p + m, 2 * m * (p + 1)
            if rev:
                rq, rk, ridx = slice(lo, mid), slice(mid, hi), mid
            else:
                rq, rk, ridx = slice(mid, hi), slice(lo, mid), mid - 1
            rrow = u[ridx:ridx + 1]
            qt = (qf[rq] * jnp.exp(u[rq] - rrow)).astype(BF16)
            kt = (kk[rk] * jnp.exp(rrow - u[rk])).astype(BF16)
            for h in range(H_HEADS):
                sl = slice(h * LANE, (h + 1) * LANE)
                att = _dot_nt(qt[:, sl], kt[:, sl]).astype(BF16)
                acc[rq, sl] += _dot(att, vb[rk, sl])
        m *= 2

    rowi = lax.broadcasted_iota(jnp.int32, (HGRN_DIAG, LANE), 0)

    def diag_block(b, carry):
        r0 = pl.multiple_of(b * HGRN_DIAG, HGRN_DIAG)
        for h in range(H_HEADS):
            sl = slice(h * LANE, (h + 1) * LANE)
            q_blk = qs[pl.ds(r0, HGRN_DIAG), sl]
            u_blk = us[pl.ds(r0, HGRN_DIAG), sl]
            k_blk = ks[pl.ds(r0, HGRN_DIAG), sl]
            v_blk = vs[pl.ds(r0, HGRN_DIAG), sl]
            o_blk = jnp.zeros((HGRN_DIAG, LANE), F32)
            for s in range(HGRN_DIAG):
                k_row = k_blk[s:s + 1]
                u_row = u_blk[s:s + 1]
                v_row = v_blk[s:s + 1]
                w = q_blk * k_row * jnp.exp(jnp.minimum(u_blk - u_row, 0.0))
                att = jnp.sum(w, axis=-1, keepdims=True)
                valid = (rowi <= s) if rev else (rowi >= s)
                o_blk = o_blk + jnp.where(valid, att, 0.0) * v_row
            acc[pl.ds(r0, HGRN_DIAG), sl] += o_blk
        return carry

    lax.fori_loop(0, n // HGRN_DIAG, diag_block, 0)

    o = acc[...]
    if not rev:
        o_ref[...] = o
    else:
        o = o + yin_ref[...]
        nw = nw_ref[...]
        for h in range(H_HEADS):
            sl = slice(h * LANE, (h + 1) * LANE)
            oh = o[:, sl]
            ms = jnp.mean(oh * oh, axis=-1, keepdims=True)
            gf = g_ref[:, sl].astype(F32)
            o_ref[:, sl] = (oh * lax.rsqrt(ms + EPS) * nw * (gf * _sigmoid(gf))).astype(o_ref.dtype)


def _hgrn(rev, nb, seq, hin, lbf, om, tri, extra):
    n = HGRN_CHUNK
    nc = seq // n
    t = nb * seq

    def blk(col):
        return lambda b, c: (b * nc + (nc - 1 - c if rev else c), col)

    const = lambda b, c: (0, 0)
    in_specs = [
        pl.BlockSpec((n, H_WIDTH), blk(0)),
        pl.BlockSpec((n, H_WIDTH), blk(2 if rev else 1)),
        pl.BlockSpec((n, H_WIDTH), blk(3)),
        pl.BlockSpec((1, H_WIDTH), const),
        pl.BlockSpec((1, H_WIDTH), const),
        pl.BlockSpec((n, n), const),
    ]
    args = [hin, hin, hin, lbf, om, tri]
    if rev:
        y_in, nw = extra
        in_specs += [pl.BlockSpec((n, H_WIDTH), blk(0)), pl.BlockSpec((n, H_WIDTH), blk(4)),
                     pl.BlockSpec((1, LANE), const)]
        args += [y_in, hin, nw]
        out_dtype = BF16
    else:
        out_dtype = F32
    return pl.pallas_call(
        functools.partial(_hgrn_kernel, rev),
        grid=(nb, nc),
        in_specs=in_specs,
        out_specs=pl.BlockSpec((n, H_WIDTH), blk(0)),
        out_shape=jax.ShapeDtypeStruct((t, H_WIDTH), out_dtype),
        scratch_shapes=[pltpu.VMEM((H_HEADS, LANE, LANE), F32)]
        + [pltpu.VMEM((n, H_WIDTH), F32) for _ in range(5)],
        compiler_params=_params(dimension_semantics=("arbitrary", "arbitrary")),
        name="hgrn_rev" if rev else "hgrn_fwd",
    )(*args)


def _attn_kernel(tq, tk, q_ref, k_ref, v_ref, o_ref, m_s, l_s, acc_s):
    nk = k_ref.shape[0] // tk
    for g in range(A_KV):
        gl = slice(g * LANE, (g + 1) * LANE)
        qs = jnp.concatenate(
            [q_ref[:, (g * A_GROUP + r) * LANE:(g * A_GROUP + r + 1) * LANE] for r in range(A_GROUP)], axis=0)
        m_s[...] = jnp.full_like(m_s, -jnp.inf)
        l_s[...] = jnp.zeros_like(l_s)
        acc_s[...] = jnp.zeros_like(acc_s)

        def body(kt, carry):
            k0 = pl.multiple_of(kt * tk, tk)
            kb = k_ref[pl.ds(k0, tk), gl]
            vb = v_ref[pl.ds(k0, tk), gl]
            s = _dot_nt(qs, kb)
            m_prev = m_s[...]
            m_new = jnp.maximum(m_prev, jnp.max(s, axis=-1, keepdims=True))
            p = jnp.exp(s - m_new[:, 0:1])
            alpha = jnp.exp(m_prev - m_new)
            l_s[...] = alpha * l_s[...] + jnp.sum(p, axis=-1, keepdims=True)
            acc_s[...] = alpha * acc_s[...] + _dot(p.astype(BF16), vb)
            m_s[...] = m_new
            return carry

        lax.fori_loop(0, nk, body, 0)
        o = acc_s[...] / l_s[...]
        heads = [o[r * tq:(r + 1) * tq] for r in range(A_GROUP)]
        for pr in range(A_GROUP // 2):
            packed = heads[2 * pr] + pltpu.roll(heads[2 * pr + 1], A_HD, 1)
            c0 = g * A_GROUP * A_HD + pr * LANE
            o_ref[:, c0:c0 + LANE] = packed.astype(o_ref.dtype)


def _attn(nb, seq, qh, kh, vh, tq, tk):
    t = nb * seq
    nq = seq // tq
    rows = A_GROUP * tq
    return pl.pallas_call(
        functools.partial(_attn_kernel, tq, tk),
        grid=(nb, nq),
        in_specs=[
            pl.BlockSpec((tq, A_HEADS * LANE), lambda b, i: (b * nq + i, 0)),
            pl.BlockSpec((seq, A_KV * LANE), lambda b, i: (b, 0)),
            pl.BlockSpec((seq, A_KV * LANE), lambda b, i: (b, 0)),
        ],
        out_specs=pl.BlockSpec((tq, A_HEADS * A_HD), lambda b, i: (b * nq + i, 0)),
        out_shape=jax.ShapeDtypeStruct((t, A_HEADS * A_HD), BF16),
        scratch_shapes=[pltpu.VMEM((rows, LANE), F32) for _ in range(3)],
        compiler_params=_params(dimension_semantics=("parallel", "parallel")),
        name="attn",
    )(qh, kh, vh)


def _merge_kernel(x_ref, ym_ref, yh_ref, ya_ref, g_ref, wb_ref, wo_ref, nw_ref, o_ref):
    mixed = None
    for i, y_ref in enumerate((ym_ref, yh_ref, ya_ref)):
        proj = _dot(y_ref[...], wb_ref[i])
        gate = _sigmoid(g_ref[:, i * D_MODEL:(i + 1) * D_MODEL].astype(F32))
        mixed = gate * proj if mixed is None else mixed + gate * proj
    out = _dot(mixed.astype(BF16), wo_ref[...])
    o_ref[...] = x_ref[...] + _rms(out, nw_ref[...])


def _merge(x, ym, yh, ya, gates, wb, wo, nw, tm):
    t = x.shape[0]
    row = lambda i: (i, 0)
    return pl.pallas_call(
        _merge_kernel,
        grid=(t // tm,),
        in_specs=[
            pl.BlockSpec((tm, D_MODEL), row),
            pl.BlockSpec((tm, M_INNER), row),
            pl.BlockSpec((tm, H_WIDTH), row),
            pl.BlockSpec((tm, A_HEADS * A_HD), row),
            pl.BlockSpec((tm, 3 * D_MODEL), row),
            pl.BlockSpec((3, M_INNER, D_MODEL), lambda i: (0, 0, 0)),
            pl.BlockSpec((D_MODEL, D_MODEL), lambda i: (0, 0)),
            pl.BlockSpec((1, D_MODEL), lambda i: (0, 0)),
        ],
        out_specs=pl.BlockSpec((tm, D_MODEL), row),
        out_shape=jax.ShapeDtypeStruct((t, D_MODEL), F32),
        compiler_params=_params(dimension_semantics=("parallel",)),
        name="merge",
    )(x, ym, yh, ya, gates, wb, wo, nw)


def _ffn_kernel(tiles_per_seq, xp_ref, xm_ref, xn_ref, nw1_ref, wup_ref, cw_ref, cb_ref, wdn_ref,
                nw2_ref, o_ref):
    j = pl.program_id(0) % tiles_per_seq
    tm = xm_ref.shape[0]
    x = xm_ref[...]
    nw1 = nw1_ref[...]
    hp = jnp.where(j > 0, _rms(xp_ref[...], nw1), 0.0)
    hn = jnp.where(j < tiles_per_seq - 1, _rms(xn_ref[...], nw1), 0.0)
    hext = jnp.concatenate([hp, _rms(x, nw1), hn], axis=0).astype(BF16)

    def conv(u, c0):
        w = cw_ref[:, c0:c0 + FFN_COLS]
        out = cb_ref[:, c0:c0 + FFN_COLS]
        for k in range(3):
            start = FFN_HALO - 1 + k
            out = out + u[start:start + tm] * w[k:k + 1]
        return out

    acc = jnp.zeros((tm, D_MODEL), F32)
    for cc in range(D_FF // FFN_COLS):
        c0 = cc * FFN_COLS
        ug = conv(_dot(hext, wup_ref[:, c0:c0 + FFN_COLS]), c0)
        uu = conv(_dot(hext, wup_ref[:, D_FF + c0:D_FF + c0 + FFN_COLS]), D_FF + c0)
        act = (ug * _sigmoid(ug) * uu).astype(BF16)
        acc = acc + _dot(act, wdn_ref[c0:c0 + FFN_COLS, :])
    o_ref[...] = x + _rms(acc, nw2_ref[...])


def _ffn(x, nw1, wup, cw, cb, wdn, nw2, seq, tm):
    t = x.shape[0]
    tiles_per_seq = seq // tm
    hb = tm // FFN_HALO
    last_halo = t // FFN_HALO - 1
    const = lambda i: (0, 0)
    single = pl.Buffered(1)
    return pl.pallas_call(
        functools.partial(_ffn_kernel, tiles_per_seq),
        grid=(t // tm,),
        in_specs=[
            pl.BlockSpec((FFN_HALO, D_MODEL), lambda i: (jnp.maximum(i * hb - 1, 0), 0)),
            pl.BlockSpec((tm, D_MODEL), lambda i: (i, 0)),
            pl.BlockSpec((FFN_HALO, D_MODEL), lambda i: (jnp.minimum(i * hb + hb, last_halo), 0)),
            pl.BlockSpec((1, D_MODEL), const),
            pl.BlockSpec((D_MODEL, 2 * D_FF), const, pipeline_mode=single),
            pl.BlockSpec((3, 2 * D_FF), const),
            pl.BlockSpec((1, 2 * D_FF), const),
            pl.BlockSpec((D_FF, D_MODEL), const, pipeline_mode=single),
            pl.BlockSpec((1, D_MODEL), const),
        ],
        out_specs=pl.BlockSpec((tm, D_MODEL), lambda i: (i, 0)),
        out_shape=jax.ShapeDtypeStruct((t, D_MODEL), F32),
        compiler_params=_params(dimension_semantics=("parallel",)),
        name="ffn",
    )(x, x, x, nw1, wup, cw, cb, wdn, nw2)


def _pad_heads(w, heads):
    d = w.shape[0]
    w = w.reshape(d, heads, A_HD)
    return jnp.pad(w, ((0, 0), (0, 0), (0, LANE - A_HD))).reshape(d, heads * LANE)


def _pack_w_in(w):
    sizes = (M_INNER, M_XBC, M_HEADS, M_HEADS, H_WIDTH, H_WIDTH, H_WIDTH, H_WIDTH, H_WIDTH,
             A_HEADS * A_HD, A_KV * A_HD, A_KV * A_HD, 3 * D_MODEL)
    parts, off = [], 0
    for s in sizes:
        parts.append(w[:, off:off + s])
        off += s
    (m_z, m_xbc, m_dtf, m_dtb, h_q, h_ff, h_fb, h_i, h_g, a_q, a_k, a_v, gate) = parts
    dt = jnp.pad(jnp.concatenate([m_dtf, m_dtb], axis=1), ((0, 0), (0, LANE - 2 * M_HEADS)))
    packed = jnp.concatenate(
        [m_z, m_xbc, dt, h_q, h_ff, h_fb, h_i, h_g,
         _pad_heads(a_q, A_HEADS), _pad_heads(a_k, A_KV), _pad_heads(a_v, A_KV), gate], axis=1)
    return packed.astype(BF16)


def _pad_lane(v):
    v = v.reshape(1, -1)
    return jnp.pad(v, ((0, 0), (0, LANE - v.shape[1])))


def _rope_tables(n):
    pos = jnp.arange(n)
    row = (pos // GRID_W).astype(F32)
    col = (pos % GRID_W).astype(F32)
    half = A_HD // 4
    inv = ROPE_BASE ** (-jnp.arange(0, 2 * half, 2, dtype=F32) / (2 * half))
    ar = row[:, None] * inv
    ac = col[:, None] * inv
    cr, sr, cc, sc = jnp.cos(ar), jnp.sin(ar), jnp.cos(ac), jnp.sin(ac)
    zeros = jnp.zeros((n, LANE - A_HD), F32)
    cos = jnp.concatenate([cr, cr, cc, cc, zeros], axis=1)
    sin = jnp.concatenate([-sr, sr, -sc, sc, zeros], axis=1)
    return cos, sin


def _tri(n, rev):
    i = jnp.arange(n)
    m = (i[None, :] >= i[:, None]) if rev else (i[None, :] <= i[:, None])
    return m.astype(BF16)


def _pick_tile(seq, want):
    return want if seq % want == 0 else seq


def kernel(x_prompt, x_sample, norm_mix_pre, w_in, m_conv_w, m_conv_b, m_dt_bias, m_a_log, m_d, m_norm_w, h_lb_logits, h_norm_w, a_q_norm, a_k_norm, w_branch, w_out, norm_mix_post, norm_ffn_pre, f_w_up, f_conv_w, f_conv_b, f_w_down, norm_ffn_post):
    sm = jax.nn.softmax(h_lb_logits.astype(F32), axis=0)
    tail = jnp.concatenate([jnp.zeros_like(sm[:1]), sm[1:]], axis=0)
    lower = jnp.cumsum(tail, axis=0)
    one_minus = sm[0:1] + (jnp.sum(tail, axis=0, keepdims=True) - lower)
    lower_floor = jnp.maximum(lower, LB_FLOOR)

    layers = []
    for l in range(DEPTH):
        layers.append(dict(
            nw_pre=norm_mix_pre[l].reshape(1, D_MODEL),
            w_in=_pack_w_in(w_in[l]),
            cw=m_conv_w[l], cb=m_conv_b[l].reshape(1, M_XBC),
            dtb=_pad_lane(m_dt_bias[l]), a_neg=_pad_lane(-jnp.exp(m_a_log[l].astype(F32))),
            dskip=jnp.repeat(m_d[l], M_INNER // M_HEADS).reshape(1, M_INNER),
            m_nw=m_norm_w[l].reshape(1, M_INNER),
            lbf=lower_floor[l], om=one_minus[l],
            h_nw=h_norm_w[l].reshape(1, LANE),
            qw=_pad_lane(a_q_norm[l] * (A_HD ** -0.5)), kw=_pad_lane(a_k_norm[l]),
            wb=w_branch[l].astype(BF16), wo=w_out[l].astype(BF16),
            nw_post=norm_mix_post[l].reshape(1, D_MODEL),
            nw_ffn=norm_ffn_pre[l].reshape(1, D_MODEL),
            wup=f_w_up[l].astype(BF16), fcw=f_conv_w[l], fcb=f_conv_b[l].reshape(1, 2 * D_FF),
            wdn=f_w_down[l].astype(BF16),
            nw_ffn_post=norm_ffn_post[l].reshape(1, D_MODEL),
        ))
    tri_ssd = (_tri(SSD_CHUNK, False), _tri(SSD_CHUNK, True))
    tri_hgrn = (_tri(HGRN_CHUNK, False), _tri(HGRN_CHUNK, True))

    def run(x):
        nb, seq, _ = x.shape
        xf = x.reshape(nb * seq, D_MODEL)
        cos, sin = _rope_tables(seq)
        tm = _pick_tile(seq, 512)
        tq = _pick_tile(seq, 256)
        tk = _pick_tile(seq, 512)
        for p in layers:
            z, xbc, dtraw, hin, qh, kh, vh, gates = _inproj(
                xf, p["nw_pre"], p["w_in"], cos, sin, p["qw"], p["kw"], seq, tm)
            ssd_args = (xbc, dtraw, p["cw"], p["cb"], p["dtb"], p["a_neg"])
            y_f = _ssd(False, nb, seq, *ssd_args, tri_ssd[0], (p["dskip"],))
            y_m = _ssd(True, nb, seq, *ssd_args, tri_ssd[1], (y_f, z, p["m_nw"]))
            o_f = _hgrn(False, nb, seq, hin, p["lbf"][0:1], p["om"][0:1], tri_hgrn[0], None)
            y_h = _hgrn(True, nb, seq, hin, p["lbf"][1:2], p["om"][1:2], tri_hgrn[1], (o_f, p["h_nw"]))
            y_a = _attn(nb, seq, qh, kh, vh, tq, tk)
            xf = _merge(xf, y_m, y_h, y_a, gates, p["wb"], p["wo"], p["nw_post"], tm)
            xf = _ffn(xf, p["nw_ffn"], p["wup"], p["fcw"], p["fcb"], p["wdn"], p["nw_ffn_post"], seq, tm)
        return xf.reshape(nb, seq, D_MODEL)

    return (run(x_prompt), run(x_sample))
```

```python
import functools

import jax
import jax.numpy as jnp
from jax import lax
from jax.experimental import pallas as pl
from jax.experimental.pallas import tpu as pltpu

F32 = jnp.float32
BF16 = jnp.bfloat16

D_MODEL = 1024
DEPTH = 4
EPS = 1e-6
LB_FLOOR = 1e-30
GRID_W = 64
ROPE_BASE = 10000.0

M_HEADS = 8
M_INNER = 512
M_STATE = 64
M_CONV = 4
M_XBC = 768
SSD_CHUNK = 128

H_HEADS = 4
H_KEY = 128
H_WIDTH = 512
HGRN_CHUNK = 128
HGRN_DIAG = 16

A_HEADS = 8
A_KV = 2
A_HD = 64
A_GROUP = A_HEADS // A_KV

D_FF = 2816
FFN_COLS = 256
FFN_HALO = 8
XBC_HALO = 16

LANE = 128
LOG2E = 1.4426950408889634
ATT_SHIFT_LIMIT = 40.0
VMEM_LIMIT = 56 * 1024 * 1024

OFF_Z = 0
OFF_XBC = OFF_Z + M_INNER
OFF_DT = OFF_XBC + M_XBC
OFF_H = OFF_DT + LANE
OFF_Q = OFF_H + 5 * H_WIDTH
OFF_K = OFF_Q + A_HEADS * LANE
OFF_V = OFF_K + A_KV * LANE
OFF_G = OFF_V + A_KV * LANE
N_PACKED = OFF_G + 3 * D_MODEL


def _rms(x, w):
    return x * lax.rsqrt(jnp.mean(x * x, axis=-1, keepdims=True) + EPS) * w


def _sigmoid(x):
    return 1.0 / (1.0 + jnp.exp(-x))


def _dot(a, b):
    return jnp.dot(a, b, preferred_element_type=F32)


def _dot_nt(a, b):
    return lax.dot_general(a, b, (((1,), (1,)), ((), ())), preferred_element_type=F32)


def _split_dot(tri, x):
    hi = x.astype(BF16)
    lo = (x - hi.astype(F32)).astype(BF16)
    return _dot(tri, hi) + _dot(tri, lo)


def _params(**kw):
    return pltpu.CompilerParams(vmem_limit_bytes=VMEM_LIMIT, **kw)


def _inproj_kernel(x_ref, nw_ref, w_ref, cos_ref, sin_ref, qw_ref, kw_ref, qadd_ref, kadd_ref, vadd_ref,
                   oz, oxbc, odt, oh, oq, ok, ov, og):
    h = _rms(x_ref[...], nw_ref[...]).astype(BF16)

    def mm(c0, n):
        return _dot(h, w_ref[:, c0:c0 + n])

    def plain(o_ref, c0, width, step):
        for j in range(0, width, step):
            o_ref[:, j:j + step] = mm(c0 + j, step).astype(o_ref.dtype)

    plain(oz, OFF_Z, M_INNER, 512)
    plain(oxbc, OFF_XBC, M_XBC, 256)
    odt[...] = mm(OFF_DT, LANE)
    plain(oh, OFF_H, 5 * H_WIDTH, 512)

    cos = cos_ref[...]
    sin = sin_ref[...]
    lane = lax.broadcasted_iota(jnp.int32, cos.shape, 1)
    first_half = (lane & 16) == 0

    def norm_rope(y, w, add):
        ss = jnp.sum(y * y, axis=-1, keepdims=True) * (1.0 / A_HD)
        yn = y * lax.rsqrt(ss + EPS) * w
        partner = jnp.where(first_half, pltpu.roll(yn, LANE - 16, 1), pltpu.roll(yn, 16, 1))
        return yn * cos + partner * sin + add

    qw = qw_ref[...]
    kw = kw_ref[...]
    qadd = qadd_ref[...]
    kadd = kadd_ref[...]
    for hd in range(A_HEADS):
        oq[:, hd * LANE:(hd + 1) * LANE] = norm_rope(mm(OFF_Q + hd * LANE, LANE), qw, qadd).astype(BF16)
    for hd in range(A_KV):
        ok[:, hd * LANE:(hd + 1) * LANE] = norm_rope(mm(OFF_K + hd * LANE, LANE), kw, kadd).astype(BF16)
    ov[...] = (mm(OFF_V, A_KV * LANE) + vadd_ref[...]).astype(BF16)
    plain(og, OFF_G, 3 * D_MODEL, 512)


def _inproj(x, nw, w, cos, sin, qw, kw, qadd, kadd, vadd, seq, tm):
    t = x.shape[0]
    tiles_per_seq = seq // tm
    row = lambda i: (i, 0)
    const = lambda i: (0, 0)
    pos = lambda i: (i % tiles_per_seq, 0)
    widths = (M_INNER, M_XBC, LANE, 5 * H_WIDTH, A_HEADS * LANE, A_KV * LANE, A_KV * LANE, 3 * D_MODEL)
    dtypes = (BF16, BF16, F32, BF16, BF16, BF16, BF16, BF16)
    return pl.pallas_call(
        _inproj_kernel,
        grid=(t // tm,),
        in_specs=[
            pl.BlockSpec((tm, D_MODEL), row),
            pl.BlockSpec((1, D_MODEL), const),
            pl.BlockSpec((D_MODEL, N_PACKED), const, pipeline_mode=pl.Buffered(1)),
            pl.BlockSpec((tm, LANE), pos),
            pl.BlockSpec((tm, LANE), pos),
            pl.BlockSpec((1, LANE), const),
            pl.BlockSpec((1, LANE), const),
            pl.BlockSpec((1, LANE), const),
            pl.BlockSpec((1, LANE), const),
            pl.BlockSpec((1, A_KV * LANE), const),
        ],
        out_specs=[pl.BlockSpec((tm, wd), row) for wd in widths],
        out_shape=[jax.ShapeDtypeStruct((t, wd), dt) for wd, dt in zip(widths, dtypes)],
        compiler_params=_params(dimension_semantics=("parallel",)),
        name="inproj",
    )(x, nw, w, cos, sin, qw, kw, qadd, kadd, vadd)


def _ssd_kernel(rev, nc, *refs):
    if rev:
        (xc_ref, dt_ref, dtb_ref, a_ref, tri_ref, yin_ref, z_ref, nw_ref, o_ref, s_ref) = refs
    else:
        (xp_ref, xm_ref, xn_ref, dt_ref, cw_ref, cb_ref, dtb_ref, a_ref, tri_ref,
         dsk_ref, o_ref, oxc_ref, s_ref) = refs
    c = pl.program_id(1)
    q = dt_ref.shape[0]

    @pl.when(c == 0)
    def _():
        s_ref[...] = jnp.zeros_like(s_ref)

    if rev:
        xc = xc_ref[...].astype(F32)
    else:
        xp = jnp.where(c > 0, xp_ref[...].astype(F32), 0.0)
        xn = jnp.where(c < nc - 1, xn_ref[...].astype(F32), 0.0)
        xe = jnp.concatenate([xp, xm_ref[...].astype(F32), xn], axis=0)
        cw = cw_ref[...]
        y = cb_ref[...]
        for k in range(M_CONV):
            start = XBC_HALO - M_CONV // 2 + k
            y = y + xe[start:start + q] * cw[k:k + 1]
        xc = y * _sigmoid(y)
        oxc_ref[...] = xc.astype(oxc_ref.dtype)
    xs = xc[:, :M_INNER]
    bm = xc[:, M_INNER:M_INNER + LANE]
    cm = xc[:, M_INNER + LANE:]

    dtr = dt_ref[...] + dtb_ref[...]
    dt = jnp.maximum(dtr, 0.0) + jnp.log1p(jnp.exp(-jnp.abs(dtr)))
    a = dt * a_ref[...]
    u = _split_dot(tri_ref[...], a)
    u_end = u[0:1] if rev else u[q - 1:q]
    w_state = dt * jnp.exp(u_end - u)
    off = jnp.exp(u)
    cd = jnp.exp(u_end)
    u_t = u.T
    dt_t = dt.T

    li = lax.broadcasted_iota(jnp.int32, (q, q), 0)
    si = lax.broadcasted_iota(jnp.int32, (q, q), 1)
    mask = (si >= li) if rev else (si <= li)
    lane_s = lax.broadcasted_iota(jnp.int32, (q, LANE), 1)
    lane_x = lax.broadcasted_iota(jnp.int32, (q, 2 * LANE), 1)
    lane_st = lax.broadcasted_iota(jnp.int32, (LANE, 2 * LANE), 1)
    lane_row = lax.broadcasted_iota(jnp.int32, (1, 2 * LANE), 1)
    base = M_HEADS if rev else 0
    hpg = M_HEADS // 2

    ys = []
    for g in range(2):
        gsel = (lane_s >= M_STATE) if g else (lane_s < M_STATE)
        cg = jnp.where(gsel, cm, 0.0)
        bg = jnp.where(gsel, bm, 0.0)
        cbm = _dot_nt(cg.astype(BF16), bm.astype(BF16))
        xg = xs[:, g * 2 * LANE:(g + 1) * 2 * LANE]
        sg = s_ref[g]
        lhs = []
        for r in range(hpg):
            ln = base + g * hpg + r
            dec = jnp.exp(jnp.where(mask, u[:, ln:ln + 1] - u_t[ln:ln + 1, :], -1e30))
            lhs.append((cbm * dec * dt_t[ln:ln + 1, :]).astype(BF16))
        for r in range(hpg):
            ln = base + g * hpg + r
            lhs.append((cg * off[:, ln:ln + 1]).astype(BF16))
        xr = [jnp.where((lane_x >> 6) == r, xg, 0.0).astype(BF16) for r in range(hpg)]
        sr = [jnp.where((lane_st >> 6) == r, sg, 0.0).astype(BF16) for r in range(hpg)]
        ys.append(_dot(jnp.concatenate(lhs, axis=1), jnp.concatenate(xr + sr, axis=0)))

        bw_t = []
        cdrow = jnp.zeros((1, 2 * LANE), F32)
        for r in range(hpg):
            ln = base + g * hpg + r
            bw_t.append((bg * w_state[:, ln:ln + 1]).T.astype(BF16))
            cdrow = jnp.where((lane_row >> 6) == r, cd[:, ln:ln + 1], cdrow)
        st = _dot(jnp.concatenate(bw_t, axis=1), jnp.concatenate(xr, axis=0))
        s_ref[g] = sg * cdrow + st

    yc = jnp.concatenate(ys, axis=1)
    if not rev:
        o_ref[...] = yc + xs * dsk_ref[...]
    else:
        yt = yin_ref[...] + yc
        zf = z_ref[...].astype(F32)
        yt = yt * (zf * _sigmoid(zf))
        nw = nw_ref[...]
        for g in range(2):
            sl = slice(g * 2 * LANE, (g + 1) * 2 * LANE)
            blk = yt[:, sl]
            ms = jnp.mean(blk * blk, axis=-1, keepdims=True)
            o_ref[:, sl] = (blk * lax.rsqrt(ms + EPS) * nw[:, sl]).astype(o_ref.dtype)


def _ssd_fwd(nb, seq, xbc, dtraw, cw, cb, dtb, a_neg, tri, dskip):
    q = SSD_CHUNK
    nc = seq // q
    t = nb * seq
    hb = q // XBC_HALO
    last_halo = t // XBC_HALO - 1
    main = lambda b, c: (b * nc + c, 0)
    prev = lambda b, c: (jnp.maximum((b * nc + c) * hb - 1, 0), 0)
    nxt = lambda b, c: (jnp.minimum((b * nc + c) * hb + hb, last_halo), 0)
    const = lambda b, c: (0, 0)
    return pl.pallas_call(
        functools.partial(_ssd_kernel, False, nc),
        grid=(nb, nc),
        in_specs=[
            pl.BlockSpec((XBC_HALO, M_XBC), prev),
            pl.BlockSpec((q, M_XBC), main),
            pl.BlockSpec((XBC_HALO, M_XBC), nxt),
            pl.BlockSpec((q, LANE), main),
            pl.BlockSpec((M_CONV, M_XBC), const),
            pl.BlockSpec((1, M_XBC), const),
            pl.BlockSpec((1, LANE), const),
            pl.BlockSpec((1, LANE), const),
            pl.BlockSpec((q, q), const),
            pl.BlockSpec((1, M_INNER), const),
        ],
        out_specs=[pl.BlockSpec((q, M_INNER), main), pl.BlockSpec((q, M_XBC), main)],
        out_shape=[jax.ShapeDtypeStruct((t, M_INNER), F32), jax.ShapeDtypeStruct((t, M_XBC), BF16)],
        scratch_shapes=[pltpu.VMEM((2, LANE, 2 * LANE), F32)],
        compiler_params=_params(dimension_semantics=("arbitrary", "arbitrary")),
        name="ssd_fwd",
    )(xbc, xbc, xbc, dtraw, cw, cb, dtb, a_neg, tri, dskip)


def _ssd_rev(nb, seq, xc, dtraw, dtb, a_neg, tri, y_in, z, nw):
    q = SSD_CHUNK
    nc = seq // q
    t = nb * seq
    main = lambda b, c: (b * nc + nc - 1 - c, 0)
    const = lambda b, c: (0, 0)
    return pl.pallas_call(
        functools.partial(_ssd_kernel, True, nc),
        grid=(nb, nc),
        in_specs=[
            pl.BlockSpec((q, M_XBC), main),
            pl.BlockSpec((q, LANE), main),
            pl.BlockSpec((1, LANE), const),
            pl.BlockSpec((1, LANE), const),
            pl.BlockSpec((q, q), const),
            pl.BlockSpec((q, M_INNER), main),
            pl.BlockSpec((q, M_INNER), main),
            pl.BlockSpec((1, M_INNER), const),
        ],
        out_specs=pl.BlockSpec((q, M_INNER), main),
        out_shape=jax.ShapeDtypeStruct((t, M_INNER), BF16),
        scratch_shapes=[pltpu.VMEM((2, LANE, 2 * LANE), F32)],
        compiler_params=_params(dimension_semantics=("arbitrary", "arbitrary")),
        name="ssd_rev",
    )(xc, dtraw, dtb, a_neg, tri, y_in, z, nw)


def _hgrn_kernel(rev, *refs):
    if rev:
        (q_ref, f_ref, i_ref, lbf_ref, om_ref, tri_ref, yin_ref, g_ref, nw_ref,
         o_ref, st_ref) = refs
    else:
        (q_ref, f_ref, i_ref, lbf_ref, om_ref, tri_ref, o_ref, st_ref) = refs
    c = pl.program_id(1)
    n = q_ref.shape[0]

    @pl.when(c == 0)
    def _():
        st_ref[...] = jnp.zeros_like(st_ref)

    raw = f_ref[...].astype(F32)
    t = jnp.exp(-jnp.abs(raw))
    r = 1.0 / (1.0 + t)
    nonneg = raw >= 0.0
    sig = jnp.where(nonneg, r, t * r)
    sig_neg = jnp.where(nonneg, t * r, r)
    om = om_ref[...]
    logf = jnp.log(lbf_ref[...] + om * sig)
    kk = om * sig_neg
    u = _split_dot(tri_ref[...], logf * LOG2E)
    qf = q_ref[...].astype(F32) * (H_KEY ** -0.5)
    v = i_ref[...].astype(F32)

    u_end = u[0:1] if rev else u[n - 1:n]
    qb = (qf * jnp.exp2(u)).astype(BF16)
    kw = (kk * jnp.exp2(u_end - u)).astype(BF16)
    cd = jnp.exp2(u_end)

    li = lax.broadcasted_iota(jnp.int32, (n, n), 0)
    si = lax.broadcasted_iota(jnp.int32, (n, n), 1)
    levels = []
    m = HGRN_DIAG
    while 2 * m <= n:
        sh = m.bit_length() - 1
        rows = []
        for p in range(n // (2 * m)):
            ridx = 2 * m * p + (m if rev else m - 1)
            rows.append(jnp.broadcast_to(u[ridx:ridx + 1], (2 * m, H_WIDTH)))
        rb = rows[0] if len(rows) == 1 else jnp.concatenate(rows, axis=0)
        qt = (qf * jnp.exp2(u - rb)).astype(BF16)
        kt = (kk * jnp.exp2(rb - u)).astype(BF16)
        same_pair = (li >> (sh + 1)) == (si >> (sh + 1))
        q_half = ((li >> sh) & 1) == (0 if rev else 1)
        k_half = ((si >> sh) & 1) == (1 if rev else 0)
        levels.append((qt, kt, same_pair & q_half & k_half))
        m *= 2

    half = HGRN_DIAG // 2
    rowi = lax.broadcasted_iota(jnp.int32, (half, LANE), 0)

    def diag_blocks(h):
        sl = slice(h * LANE, (h + 1) * LANE)
        pieces = []
        for b in range(n // HGRN_DIAG):
            r0 = b * HGRN_DIAG
            rows = (slice(r0, r0 + half), slice(r0 + half, r0 + HGRN_DIAG))
            q_h = (qf[rows[0], sl], qf[rows[1], sl])
            u_h = (u[rows[0], sl], u[rows[1], sl])
            o_h = [None, None]
            for s in range(HGRN_DIAG):
                k_row = kk[r0 + s:r0 + s + 1, sl]
                u_row = u[r0 + s:r0 + s + 1, sl]
                v_row = v[r0 + s:r0 + s + 1, sl]
                own = s // half
                for part in range(2):
                    if (part > own) if rev else (part < own):
                        continue
                    w = q_h[part] * (k_row * jnp.exp2(u_h[part] - u_row))
                    att = jnp.sum(w, axis=-1, keepdims=True)
                    if part == own:
                        local = s - own * half
                        valid = (rowi <= local) if rev else (rowi >= local)
                        att = jnp.where(valid, att, 0.0)
                    o_h[part] = att * v_row if o_h[part] is None else o_h[part] + att * v_row
            pieces += o_h
        return jnp.concatenate(pieces, axis=0)

    for h in range(H_HEADS):
        sl = slice(h * LANE, (h + 1) * LANE)
        att = None
        for qt, kt, msk in levels:
            part = jnp.where(msk, _dot_nt(qt[:, sl], kt[:, sl]), 0.0)
            att = part if att is None else att + part
        st = st_ref[h]
        v_t = v[:, sl].T.astype(BF16)
        lhs = jnp.concatenate([att.astype(BF16), qb[:, sl]], axis=1)
        rhs = jnp.concatenate([v_t, st.astype(BF16)], axis=1)
        oh = _dot_nt(lhs, rhs) + diag_blocks(h)
        st_ref[h] = st * cd[:, sl] + _dot(v_t, kw[:, sl])
        if not rev:
            o_ref[:, sl] = oh
        else:
            oh = oh + yin_ref[:, sl]
            ms = jnp.mean(oh * oh, axis=-1, keepdims=True)
            gf = g_ref[:, sl].astype(F32)
            o_ref[:, sl] = (oh * lax.rsqrt(ms + EPS) * nw_ref[...] * (gf * _sigmoid(gf))).astype(o_ref.dtype)


def _hgrn(rev, nb, seq, hin, lbf, om, tri, extra):
    n = HGRN_CHUNK
    nc = seq // n
    t = nb * seq

    def blk(col):
        return lambda b, c: (b * nc + (nc - 1 - c if rev else c), col)

    const = lambda b, c: (0, 0)
    in_specs = [
        pl.BlockSpec((n, H_WIDTH), blk(0)),
        pl.BlockSpec((n, H_WIDTH), blk(2 if rev else 1)),
        pl.BlockSpec((n, H_WIDTH), blk(3)),
        pl.BlockSpec((1, H_WIDTH), const),
        pl.BlockSpec((1, H_WIDTH), const),
        pl.BlockSpec((n, n), const),
    ]
    args = [hin, hin, hin, lbf, om, tri]
    if rev:
        y_in, nw = extra
        in_specs += [pl.BlockSpec((n, H_WIDTH), blk(0)), pl.BlockSpec((n, H_WIDTH), blk(4)),
                     pl.BlockSpec((1, LANE), const)]
        args += [y_in, hin, nw]
        out_dtype = BF16
    else:
        out_dtype = F32
    return pl.pallas_call(
        functools.partial(_hgrn_kernel, rev),
        grid=(nb, nc),
        in_specs=in_specs,
        out_specs=pl.BlockSpec((n, H_WIDTH), blk(0)),
        out_shape=jax.ShapeDtypeStruct((t, H_WIDTH), out_dtype),
        scratch_shapes=[pltpu.VMEM((H_HEADS, LANE, LANE), F32)],
        compiler_params=_params(dimension_semantics=("arbitrary", "arbitrary")),
        name="hgrn_rev" if rev else "hgrn_fwd",
    )(*args)


def _attn_kernel(tq, tk, bounded_ref, q_ref, k_ref, v_ref, o_ref, m_s, acc_s):
    nk = k_ref.shape[0] // tk
    lane = lax.broadcasted_iota(jnp.int32, acc_s.shape, 1)
    for g in range(A_KV):
        gl = slice(g * LANE, (g + 1) * LANE)
        qs = jnp.concatenate(
            [q_ref[:, (g * A_GROUP + r) * LANE:(g * A_GROUP + r + 1) * LANE] for r in range(A_GROUP)], axis=0)
        acc_s[...] = jnp.zeros_like(acc_s)

        def tiles(kt):
            k0 = pl.multiple_of(kt * tk, tk)
            return k_ref[pl.ds(k0, tk), gl], v_ref[pl.ds(k0, tk), gl]

        def shifted_body(kt, carry):
            kb, vb = tiles(kt)
            acc_s[...] += _dot(jnp.exp2(_dot_nt(qs, kb)).astype(BF16), vb)
            return carry

        def online_body(kt, carry):
            kb, vb = tiles(kt)
            s = _dot_nt(qs, kb)
            m_prev = m_s[...]
            m_new = jnp.maximum(m_prev, jnp.max(s, axis=-1, keepdims=True))
            p = jnp.exp2(s - m_new[:, 0:1])
            acc_s[...] = jnp.exp2(m_prev - m_new) * acc_s[...] + _dot(p.astype(BF16), vb)
            m_s[...] = m_new
            return carry

        @pl.when(bounded_ref[0] != 0)
        def _():
            lax.fori_loop(0, nk, shifted_body, 0)

        @pl.when(bounded_ref[0] == 0)
        def _():
            m_s[...] = jnp.full_like(m_s, -jnp.inf)
            lax.fori_loop(0, nk, online_body, 0)

        o = acc_s[...]
        o = jnp.where(lane < A_HD, o / o[:, A_HD:A_HD + 1], 0.0)
        heads = [o[r * tq:(r + 1) * tq] for r in range(A_GROUP)]
        for pr in range(A_GROUP // 2):
            packed = heads[2 * pr] + pltpu.roll(heads[2 * pr + 1], A_HD, 1)
            c0 = g * A_GROUP * A_HD + pr * LANE
            o_ref[:, c0:c0 + LANE] = packed.astype(o_ref.dtype)


def _attn(nb, seq, bounded, qh, kh, vh, tq, tk):
    t = nb * seq
    nq = seq // tq
    rows = A_GROUP * tq
    return pl.pallas_call(
        functools.partial(_attn_kernel, tq, tk),
        grid=(nb, nq),
        in_specs=[
            pl.BlockSpec(memory_space=pltpu.SMEM),
            pl.BlockSpec((tq, A_HEADS * LANE), lambda b, i: (b * nq + i, 0)),
            pl.BlockSpec((seq, A_KV * LANE), lambda b, i: (b, 0)),
            pl.BlockSpec((seq, A_KV * LANE), lambda b, i: (b, 0)),
        ],
        out_specs=pl.BlockSpec((tq, A_HEADS * A_HD), lambda b, i: (b * nq + i, 0)),
        out_shape=jax.ShapeDtypeStruct((t, A_HEADS * A_HD), BF16),
        scratch_shapes=[pltpu.VMEM((rows, LANE), F32) for _ in range(2)],
        compiler_params=_params(dimension_semantics=("parallel", "parallel")),
        name="attn",
    )(bounded, qh, kh, vh)


def _merge_kernel(x_ref, ym_ref, yh_ref, ya_ref, g_ref, wb_ref, wo_ref, nw_ref, o_ref):
    mixed = None
    for i, y_ref in enumerate((ym_ref, yh_ref, ya_ref)):
        proj = _dot(y_ref[...], wb_ref[i])
        gate = _sigmoid(g_ref[:, i * D_MODEL:(i + 1) * D_MODEL].astype(F32))
        mixed = gate * proj if mixed is None else mixed + gate * proj
    out = _dot(mixed.astype(BF16), wo_ref[...])
    o_ref[...] = x_ref[...] + _rms(out, nw_ref[...])


def _merge(x, ym, yh, ya, gates, wb, wo, nw, tm):
    t = x.shape[0]
    row = lambda i: (i, 0)
    return pl.pallas_call(
        _merge_kernel,
        grid=(t // tm,),
        in_specs=[
            pl.BlockSpec((tm, D_MODEL), row),
            pl.BlockSpec((tm, M_INNER), row),
            pl.BlockSpec((tm, H_WIDTH), row),
            pl.BlockSpec((tm, A_HEADS * A_HD), row),
            pl.BlockSpec((tm, 3 * D_MODEL), row),
            pl.BlockSpec((3, M_INNER, D_MODEL), lambda i: (0, 0, 0)),
            pl.BlockSpec((D_MODEL, D_MODEL), lambda i: (0, 0)),
            pl.BlockSpec((1, D_MODEL), lambda i: (0, 0)),
        ],
        out_specs=pl.BlockSpec((tm, D_MODEL), row),
        out_shape=jax.ShapeDtypeStruct((t, D_MODEL), F32),
        compiler_params=_params(dimension_semantics=("parallel",)),
        name="merge",
    )(x, ym, yh, ya, gates, wb, wo, nw)


def _ffn_kernel(tiles_per_seq, xp_ref, xm_ref, xn_ref, nw1_ref, wup_ref, cw_ref, cb_ref, wdn_ref,
                nw2_ref, o_ref, u_s):
    j = pl.program_id(0) % tiles_per_seq
    tm = xm_ref.shape[0]
    x = xm_ref[...]
    nw1 = nw1_ref[...]
    hp = jnp.where(j > 0, _rms(xp_ref[...], nw1), 0.0)
    hn = jnp.where(j < tiles_per_seq - 1, _rms(xn_ref[...], nw1), 0.0)
    hext = jnp.concatenate([hp, _rms(x, nw1), hn], axis=0).astype(BF16)

    def up(slot, c0):
        u_s[slot] = _dot(hext, wup_ref[:, c0:c0 + FFN_COLS])

    def conv(slot, c0):
        w = cw_ref[:, c0:c0 + FFN_COLS]
        out = cb_ref[:, c0:c0 + FFN_COLS]
        for k in range(3):
            out = out + u_s[slot, pl.ds(FFN_HALO - 1 + k, tm), :] * w[k:k + 1]
        return out

    n_chunks = D_FF // FFN_COLS
    acc = jnp.zeros((tm, D_MODEL), F32)
    up(0, 0)
    up(1, D_FF)
    for cc in range(n_chunks):
        c0 = cc * FFN_COLS
        par = 2 * (cc % 2)
        if cc + 1 < n_chunks:
            up(2 - par, c0 + FFN_COLS)
            up(3 - par, D_FF + c0 + FFN_COLS)
        ug = conv(par, c0)
        uu = conv(par + 1, D_FF + c0)
        act = (ug * _sigmoid(ug) * uu).astype(BF16)
        acc = acc + _dot(act, wdn_ref[c0:c0 + FFN_COLS, :])
    o_ref[...] = x + _rms(acc, nw2_ref[...])


def _ffn(x, nw1, wup, cw, cb, wdn, nw2, seq, tm):
    t = x.shape[0]
    tiles_per_seq = seq // tm
    hb = tm // FFN_HALO
    last_halo = t // FFN_HALO - 1
    const = lambda i: (0, 0)
    single = pl.Buffered(1)
    return pl.pallas_call(
        functools.partial(_ffn_kernel, tiles_per_seq),
        grid=(t // tm,),
        in_specs=[
            pl.BlockSpec((FFN_HALO, D_MODEL), lambda i: (jnp.maximum(i * hb - 1, 0), 0)),
            pl.BlockSpec((tm, D_MODEL), lambda i: (i, 0)),
            pl.BlockSpec((FFN_HALO, D_MODEL), lambda i: (jnp.minimum(i * hb + hb, last_halo), 0)),
            pl.BlockSpec((1, D_MODEL), const),
            pl.BlockSpec((D_MODEL, 2 * D_FF), const, pipeline_mode=single),
            pl.BlockSpec((3, 2 * D_FF), const),
            pl.BlockSpec((1, 2 * D_FF), const),
            pl.BlockSpec((D_FF, D_MODEL), const, pipeline_mode=single),
            pl.BlockSpec((1, D_MODEL), const),
        ],
        out_specs=pl.BlockSpec((tm, D_MODEL), lambda i: (i, 0)),
        out_shape=jax.ShapeDtypeStruct((t, D_MODEL), F32),
        scratch_shapes=[pltpu.VMEM((4, tm + 2 * FFN_HALO, FFN_COLS), F32)],
        compiler_params=_params(dimension_semantics=("parallel",)),
        name="ffn",
    )(x, x, x, nw1, wup, cw, cb, wdn, nw2)


def _pad_heads(w, heads):
    d = w.shape[0]
    w = w.reshape(d, heads, A_HD)
    return jnp.pad(w, ((0, 0), (0, 0), (0, LANE - A_HD))).reshape(d, heads * LANE)


def _pack_w_in(w):
    sizes = (M_INNER, M_XBC, M_HEADS, M_HEADS, H_WIDTH, H_WIDTH, H_WIDTH, H_WIDTH, H_WIDTH,
             A_HEADS * A_HD, A_KV * A_HD, A_KV * A_HD, 3 * D_MODEL)
    parts, off = [], 0
    for s in sizes:
        parts.append(w[:, off:off + s])
        off += s
    (m_z, m_xbc, m_dtf, m_dtb, h_q, h_ff, h_fb, h_i, h_g, a_q, a_k, a_v, gate) = parts
    dt = jnp.pad(jnp.concatenate([m_dtf, m_dtb], axis=1), ((0, 0), (0, LANE - 2 * M_HEADS)))
    packed = jnp.concatenate(
        [m_z, m_xbc, dt, h_q, h_ff, h_fb, h_i, h_g,
         _pad_heads(a_q, A_HEADS), _pad_heads(a_k, A_KV), _pad_heads(a_v, A_KV), gate], axis=1)
    return packed.astype(BF16)


def _pad_lane(v):
    v = v.reshape(1, -1)
    return jnp.pad(v, ((0, 0), (0, LANE - v.shape[1])))


def _rope_tables(n):
    pos = jnp.arange(n)
    row = (pos // GRID_W).astype(F32)
    col = (pos % GRID_W).astype(F32)
    half = A_HD // 4
    inv = ROPE_BASE ** (-jnp.arange(0, 2 * half, 2, dtype=F32) / (2 * half))
    ar = row[:, None] * inv
    ac = col[:, None] * inv
    cr, sr, cc, sc = jnp.cos(ar), jnp.sin(ar), jnp.cos(ac), jnp.sin(ac)
    zeros = jnp.zeros((n, LANE - A_HD), F32)
    cos = jnp.concatenate([cr, cr, cc, cc, zeros], axis=1)
    sin = jnp.concatenate([-sr, sr, -sc, sc, zeros], axis=1)
    return cos, sin


def _tri(n, rev):
    i = jnp.arange(n)
    m = (i[None, :] >= i[:, None]) if rev else (i[None, :] <= i[:, None])
    return m.astype(BF16)


def _pick_tile(seq, want):
    return want if seq % want == 0 else seq


def kernel(x_prompt, x_sample, norm_mix_pre, w_in, m_conv_w, m_conv_b, m_dt_bias, m_a_log, m_d, m_norm_w, h_lb_logits, h_norm_w, a_q_norm, a_k_norm, w_branch, w_out, norm_mix_post, norm_ffn_pre, f_w_up, f_conv_w, f_conv_b, f_w_down, norm_ffn_post):
    sm = jax.nn.softmax(h_lb_logits.astype(F32), axis=0)
    tail = jnp.concatenate([jnp.zeros_like(sm[:1]), sm[1:]], axis=0)
    lower = jnp.cumsum(tail, axis=0)
    one_minus = sm[0:1] + (jnp.sum(tail, axis=0, keepdims=True) - lower)
    lower_floor = jnp.maximum(lower, LB_FLOOR)

    lane_a = jnp.arange(LANE) == A_HD
    kadd = lane_a.astype(F32).reshape(1, LANE)
    vadd = jnp.tile(kadd, (1, A_KV))
    layers = []
    for l in range(DEPTH):
        score_bound = A_HD ** 0.5 * jnp.max(jnp.abs(a_q_norm[l])) * jnp.max(jnp.abs(a_k_norm[l]))
        bounded = score_bound <= ATT_SHIFT_LIMIT
        shift = jnp.where(bounded, score_bound * LOG2E, 0.0)
        layers.append(dict(
            bounded=bounded.astype(jnp.int32).reshape(1),
            qadd=jnp.where(lane_a, -shift, 0.0).astype(F32).reshape(1, LANE), kadd=kadd, vadd=vadd,
            nw_pre=norm_mix_pre[l].reshape(1, D_MODEL),
            w_in=_pack_w_in(w_in[l]),
            cw=m_conv_w[l], cb=m_conv_b[l].reshape(1, M_XBC),
            dtb=_pad_lane(m_dt_bias[l]), a_neg=_pad_lane(-jnp.exp(m_a_log[l].astype(F32))),
            dskip=jnp.repeat(m_d[l], M_INNER // M_HEADS).reshape(1, M_INNER),
            m_nw=m_norm_w[l].reshape(1, M_INNER),
            lbf=lower_floor[l], om=one_minus[l],
            h_nw=h_norm_w[l].reshape(1, LANE),
            qw=_pad_lane(a_q_norm[l] * (A_HD ** -0.5 * LOG2E)), kw=_pad_lane(a_k_norm[l]),
            wb=w_branch[l].astype(BF16), wo=w_out[l].astype(BF16),
            nw_post=norm_mix_post[l].reshape(1, D_MODEL),
            nw_ffn=norm_ffn_pre[l].reshape(1, D_MODEL),
            wup=f_w_up[l].astype(BF16), fcw=f_conv_w[l], fcb=f_conv_b[l].reshape(1, 2 * D_FF),
            wdn=f_w_down[l].astype(BF16),
            nw_ffn_post=norm_ffn_post[l].reshape(1, D_MODEL),
        ))
    tri_ssd = (_tri(SSD_CHUNK, False), _tri(SSD_CHUNK, True))
    tri_hgrn = (_tri(HGRN_CHUNK, False), _tri(HGRN_CHUNK, True))

    def run(x):
        nb, seq, _ = x.shape
        xf = x.reshape(nb * seq, D_MODEL)
        cos, sin = _rope_tables(seq)
        tm = _pick_tile(seq, 512)
        tq = _pick_tile(seq, 256)
        tk = _pick_tile(seq, 512)
        for p in layers:
            z, xbc, dtraw, hin, qh, kh, vh, gates = _inproj(
                xf, p["nw_pre"], p["w_in"], cos, sin, p["qw"], p["kw"],
                p["qadd"], p["kadd"], p["vadd"], seq, tm)
            y_f, xc = _ssd_fwd(nb, seq, xbc, dtraw, p["cw"], p["cb"], p["dtb"], p["a_neg"],
                               tri_ssd[0], p["dskip"])
            y_m = _ssd_rev(nb, seq, xc, dtraw, p["dtb"], p["a_neg"], tri_ssd[1], y_f, z, p["m_nw"])
            o_f = _hgrn(False, nb, seq, hin, p["lbf"][0:1], p["om"][0:1], tri_hgrn[0], None)
            y_h = _hgrn(True, nb, seq, hin, p["lbf"][1:2], p["om"][1:2], tri_hgrn[1], (o_f, p["h_nw"]))
            y_a = _attn(nb, seq, p["bounded"], qh, kh, vh, tq, tk)
            xf = _merge(xf, y_m, y_h, y_a, gates, p["wb"], p["wo"], p["nw_post"], tm)
            xf = _ffn(xf, p["nw_ffn"], p["wup"], p["fcw"], p["fcb"], p["wdn"], p["nw_ffn_post"], seq, tm)
        return xf.reshape(nb, seq, D_MODEL)

    return (run(x_prompt), run(x_sample))
```

```python
import functools

import jax
import jax.numpy as jnp
from jax import lax
from jax.experimental import pallas as pl
from jax.experimental.pallas import tpu as pltpu

F32 = jnp.float32
BF16 = jnp.bfloat16

D_MODEL = 1024
DEPTH = 4
EPS = 1e-6
LB_FLOOR = 1e-30
GRID_W = 64
ROPE_BASE = 10000.0

M_HEADS = 8
M_INNER = 512
M_STATE = 64
M_CONV = 4
M_XBC = 768
SSD_CHUNK = 128
CHUNKS_PER_STEP = 4

H_HEADS = 4
H_KEY = 128
H_WIDTH = 512
HGRN_CHUNK = 128
HGRN_DIAG = 8

A_HEADS = 8
A_KV = 2
A_HD = 64
A_GROUP = A_HEADS // A_KV

D_FF = 2816
FFN_COLS = 256
FFN_HALO = 8
XBC_HALO = 16

LANE = 128
LOG2E = 1.4426950408889634
ATT_SHIFT_LIMIT = 40.0
VMEM_LIMIT = 56 * 1024 * 1024

OFF_Z = 0
OFF_XBC = OFF_Z + M_INNER
OFF_DT = OFF_XBC + M_XBC
OFF_H = OFF_DT + LANE
OFF_Q = OFF_H + 5 * H_WIDTH
OFF_K = OFF_Q + A_HEADS * A_HD
OFF_V = OFF_K + A_KV * A_HD
OFF_G = OFF_V + A_KV * A_HD
N_PACKED = OFF_G + 3 * D_MODEL


def _rms(x, w):
    return x * lax.rsqrt(jnp.mean(x * x, axis=-1, keepdims=True) + EPS) * w


def _sigmoid(x):
    return 1.0 / (1.0 + jnp.exp(-x))


def _dot(a, b):
    return jnp.dot(a, b, preferred_element_type=F32)


def _dot_nt(a, b):
    return lax.dot_general(a, b, (((1,), (1,)), ((), ())), preferred_element_type=F32)


def _split_dot(tri, x):
    hi = x.astype(BF16)
    lo = (x - hi.astype(F32)).astype(BF16)
    return _dot(tri, hi) + _dot(tri, lo)


def _params(**kw):
    return pltpu.CompilerParams(vmem_limit_bytes=VMEM_LIMIT, **kw)


def _inproj_kernel(x_ref, nw_ref, w_ref, cos_ref, sin_ref, qw_ref, kw_ref, qadd_ref, kadd_ref,
                   oz, oxbc, odt, oh, oq, ok, ov, og):
    h = _rms(x_ref[...], nw_ref[...]).astype(BF16)

    def mm(c0, n):
        return _dot(h, w_ref[:, c0:c0 + n])

    def plain(o_ref, c0, width, step):
        for j in range(0, width, step):
            o_ref[:, j:j + step] = mm(c0 + j, step).astype(o_ref.dtype)

    plain(oz, OFF_Z, M_INNER, 512)
    plain(oxbc, OFF_XBC, M_XBC, 256)
    odt[...] = mm(OFF_DT, LANE)
    plain(oh, OFF_H, 5 * H_WIDTH, 512)

    cos = cos_ref[...]
    sin = sin_ref[...]
    lane = lax.broadcasted_iota(jnp.int32, cos.shape, 1)
    first_half = (lane & 16) == 0
    low = lane < A_HD

    def split_pair(y, add):
        first = jnp.where(low, y, 0.0) + add
        second = jnp.where(low, pltpu.roll(y, A_HD, 1), 0.0) + add
        return first.astype(BF16), second.astype(BF16)

    def norm_rope(y, w):
        sq = y * y
        ss_lo = jnp.sum(jnp.where(low, sq, 0.0), axis=-1, keepdims=True)
        ss_hi = jnp.sum(jnp.where(low, 0.0, sq), axis=-1, keepdims=True)
        yn = y * lax.rsqrt(jnp.where(low, ss_lo, ss_hi) * (1.0 / A_HD) + EPS) * w
        partner = jnp.where(first_half, pltpu.roll(yn, LANE - 16, 1), pltpu.roll(yn, 16, 1))
        return yn * cos + partner * sin

    qw = qw_ref[...]
    qadd = qadd_ref[...]
    kv_add = kadd_ref[...]
    yq = mm(OFF_Q, A_HEADS * A_HD)
    for pr in range(A_HEADS // 2):
        a, b = split_pair(norm_rope(yq[:, pr * LANE:(pr + 1) * LANE], qw), qadd)
        oq[:, 2 * pr * LANE:(2 * pr + 1) * LANE] = a
        oq[:, (2 * pr + 1) * LANE:(2 * pr + 2) * LANE] = b
    ykv = mm(OFF_K, 2 * A_KV * A_HD)
    ok[:, :LANE], ok[:, LANE:] = split_pair(norm_rope(ykv[:, :LANE], kw_ref[...]), kv_add)
    ov[:, :LANE], ov[:, LANE:] = split_pair(ykv[:, LANE:], kv_add)
    plain(og, OFF_G, 3 * D_MODEL, 512)


def _inproj(x, nw, w, cos, sin, qw, kw, qadd, kadd, seq, tm):
    t = x.shape[0]
    tiles_per_seq = seq // tm
    row = lambda i: (i, 0)
    const = lambda i: (0, 0)
    pos = lambda i: (i % tiles_per_seq, 0)
    widths = (M_INNER, M_XBC, LANE, 5 * H_WIDTH, A_HEADS * LANE, A_KV * LANE, A_KV * LANE, 3 * D_MODEL)
    dtypes = (BF16, BF16, F32, BF16, BF16, BF16, BF16, BF16)
    return pl.pallas_call(
        _inproj_kernel,
        grid=(t // tm,),
        in_specs=[
            pl.BlockSpec((tm, D_MODEL), row),
            pl.BlockSpec((1, D_MODEL), const),
            pl.BlockSpec((D_MODEL, N_PACKED), const, pipeline_mode=pl.Buffered(1)),
            pl.BlockSpec((tm, LANE), pos),
            pl.BlockSpec((tm, LANE), pos),
            pl.BlockSpec((1, LANE), const),
            pl.BlockSpec((1, LANE), const),
            pl.BlockSpec((1, LANE), const),
            pl.BlockSpec((1, LANE), const),
        ],
        out_specs=[pl.BlockSpec((tm, wd), row) for wd in widths],
        out_shape=[jax.ShapeDtypeStruct((t, wd), dt) for wd, dt in zip(widths, dtypes)],
        compiler_params=_params(dimension_semantics=("parallel",)),
        name="inproj",
    )(x, nw, w, cos, sin, qw, kw, qadd, kadd)


def _ssd_kernel(rev, nblk, sub, *refs):
    if rev:
        (xc_ref, dt_ref, dtb_ref, a_ref, tri_ref, e_ref, yin_ref, z_ref, nw_ref, o_ref, s_ref) = refs
    else:
        (xp_ref, xm_ref, xn_ref, dt_ref, cw_ref, cb_ref, dtb_ref, a_ref, tri_ref, e_ref,
         dsk_ref, o_ref, oxc_ref, s_ref, xc_s) = refs
    c = pl.program_id(1)
    q = SSD_CHUNK

    @pl.when(c == 0)
    def _():
        s_ref[...] = jnp.zeros_like(s_ref)

    if not rev:
        xp = jnp.where(c > 0, xp_ref[...].astype(F32), 0.0)
        xn = jnp.where(c < nblk - 1, xn_ref[...].astype(F32), 0.0)
        xe = jnp.concatenate([xp, xm_ref[...].astype(F32), xn], axis=0)
        cw = cw_ref[...]
        y = cb_ref[...]
        for k in range(M_CONV):
            start = XBC_HALO - M_CONV // 2 + k
            y = y + xe[start:start + sub * q] * cw[k:k + 1]
        xc_all = y * _sigmoid(y)
        xc_s[...] = xc_all
        oxc_ref[...] = xc_all.astype(oxc_ref.dtype)

    li = lax.broadcasted_iota(jnp.int32, (q, q), 0)
    si = lax.broadcasted_iota(jnp.int32, (q, q), 1)
    mask = (si >= li) if rev else (si <= li)
    lane_s = lax.broadcasted_iota(jnp.int32, (q, LANE), 1)
    lane_x = lax.broadcasted_iota(jnp.int32, (q, 2 * LANE), 1)
    base = M_HEADS if rev else 0
    hpg = M_HEADS // 2

    def chunk(j, carry):
        r0 = pl.multiple_of(((sub - 1 - j) if rev else j) * q, q)
        rows = pl.ds(r0, q)
        xc = xc_ref[rows, :].astype(F32) if rev else xc_s[rows, :]
        xs = xc[:, :M_INNER]
        bm = xc[:, M_INNER:M_INNER + LANE]
        cm = xc[:, M_INNER + LANE:]

        dtr = dt_ref[rows, :] + dtb_ref[...]
        dt = jnp.maximum(dtr, 0.0) + jnp.log1p(jnp.exp(-jnp.abs(dtr)))
        a = dt * a_ref[...]
        u = _split_dot(tri_ref[...], a)
        wide = _expand_heads(jnp.concatenate([u, dt], axis=0), e_ref[...])
        u_w = wide[:q]
        dt_w = wide[q:]
        u_end_w = u_w[0:1] if rev else u_w[q - 1:q]
        off_w = jnp.exp2(u_w)
        w_state_w = dt_w * jnp.exp2(u_end_w - u_w)
        cd_w = jnp.exp2(u_end_w)
        u_t = u.T
        dt_t = dt.T

        ys = []
        for g in range(2):
            sl = slice(g * 2 * LANE, (g + 1) * 2 * LANE)
            gsel = (lane_s >= M_STATE) if g else (lane_s < M_STATE)
            cg = jnp.where(gsel, cm, 0.0).astype(BF16)
            bg = jnp.where(gsel, bm, 0.0).astype(BF16)
            cbm = _dot_nt(cg, bm.astype(BF16))
            xg = xs[:, sl]
            lhs = []
            for r in range(hpg):
                ln = base + g * hpg + r
                dec = jnp.exp2(jnp.where(mask, u[:, ln:ln + 1] - u_t[ln:ln + 1, :], -1e30))
                lhs.append((cbm * dec * dt_t[ln:ln + 1, :]).astype(BF16))
            xr = [jnp.where((lane_x >> 6) == r, xg, 0.0).astype(BF16) for r in range(hpg)]
            y_diag = _dot(jnp.concatenate(lhs, axis=1), jnp.concatenate(xr, axis=0))
            sg = s_ref[g]
            ys.append(y_diag + _dot(cg, sg.astype(BF16)) * off_w[:, sl])
            st = lax.dot_general(bg, (xg * w_state_w[:, sl]).astype(BF16),
                                 (((0,), (0,)), ((), ())), preferred_element_type=F32)
            s_ref[g] = sg * cd_w[:, sl] + st
        yc = jnp.concatenate(ys, axis=1)
        if not rev:
            o_ref[rows, :] = yc + xs * dsk_ref[...]
        else:
            yt = yin_ref[rows, :] + yc
            zf = z_ref[rows, :].astype(F32)
            yt = yt * (zf * _sigmoid(zf))
            nw = nw_ref[...]
            for g in range(2):
                sl = slice(g * 2 * LANE, (g + 1) * 2 * LANE)
                blk = yt[:, sl]
                ms = jnp.mean(blk * blk, axis=-1, keepdims=True)
                o_ref[rows, sl] = (blk * lax.rsqrt(ms + EPS) * nw[:, sl]).astype(o_ref.dtype)
        return carry

    lax.fori_loop(0, sub, chunk, 0, unroll=True)


def _expand_heads(x, e):
    hi = x.astype(BF16)
    rest = x - hi.astype(F32)
    mid = rest.astype(BF16)
    lo = (rest - mid.astype(F32)).astype(BF16)
    return _dot(hi, e) + _dot(mid, e) + _dot(lo, e)


def _chunks_per_step(n_chunks, want):
    while n_chunks % want:
        want -= 1
    return want


def _ssd_fwd(nb, seq, xbc, dtraw, cw, cb, dtb, a_neg, tri, expand, dskip):
    q = SSD_CHUNK
    sub = _chunks_per_step(seq // q, CHUNKS_PER_STEP)
    rows = sub * q
    nblk = seq // rows
    t = nb * seq
    hb = rows // XBC_HALO
    last_halo = t // XBC_HALO - 1
    main = lambda b, c: (b * nblk + c, 0)
    prev = lambda b, c: (jnp.maximum((b * nblk + c) * hb - 1, 0), 0)
    nxt = lambda b, c: (jnp.minimum((b * nblk + c) * hb + hb, last_halo), 0)
    const = lambda b, c: (0, 0)
    return pl.pallas_call(
        functools.partial(_ssd_kernel, False, nblk, sub),
        grid=(nb, nblk),
        in_specs=[
            pl.BlockSpec((XBC_HALO, M_XBC), prev),
            pl.BlockSpec((rows, M_XBC), main),
            pl.BlockSpec((XBC_HALO, M_XBC), nxt),
            pl.BlockSpec((rows, LANE), main),
            pl.BlockSpec((M_CONV, M_XBC), const),
            pl.BlockSpec((1, M_XBC), const),
            pl.BlockSpec((1, LANE), const),
            pl.BlockSpec((1, LANE), const),
            pl.BlockSpec((q, q), const),
            pl.BlockSpec((LANE, M_INNER), const),
            pl.BlockSpec((1, M_INNER), const),
        ],
        out_specs=[pl.BlockSpec((rows, M_INNER), main), pl.BlockSpec((rows, M_XBC), main)],
        out_shape=[jax.ShapeDtypeStruct((t, M_INNER), F32), jax.ShapeDtypeStruct((t, M_XBC), BF16)],
        scratch_shapes=[pltpu.VMEM((2, LANE, 2 * LANE), F32), pltpu.VMEM((rows, M_XBC), F32)],
        compiler_params=_params(dimension_semantics=("arbitrary", "arbitrary")),
        name="ssd_fwd",
    )(xbc, xbc, xbc, dtraw, cw, cb, dtb, a_neg, tri, expand, dskip)


def _ssd_rev(nb, seq, xc, dtraw, dtb, a_neg, tri, expand, y_in, z, nw):
    q = SSD_CHUNK
    sub = _chunks_per_step(seq // q, CHUNKS_PER_STEP)
    rows = sub * q
    nblk = seq // rows
    t = nb * seq
    main = lambda b, c: (b * nblk + nblk - 1 - c, 0)
    const = lambda b, c: (0, 0)
    return pl.pallas_call(
        functools.partial(_ssd_kernel, True, nblk, sub),
        grid=(nb, nblk),
        in_specs=[
            pl.BlockSpec((rows, M_XBC), main),
            pl.BlockSpec((rows, LANE), main),
            pl.BlockSpec((1, LANE), const),
            pl.BlockSpec((1, LANE), const),
            pl.BlockSpec((q, q), const),
            pl.BlockSpec((LANE, M_INNER), const),
            pl.BlockSpec((rows, M_INNER), main),
            pl.BlockSpec((rows, M_INNER), main),
            pl.BlockSpec((1, M_INNER), const),
        ],
        out_specs=pl.BlockSpec((rows, M_INNER), main),
        out_shape=jax.ShapeDtypeStruct((t, M_INNER), BF16),
        scratch_shapes=[pltpu.VMEM((2, LANE, 2 * LANE), F32)],
        compiler_params=_params(dimension_semantics=("arbitrary", "arbitrary")),
        name="ssd_rev",
    )(xc, dtraw, dtb, a_neg, tri, expand, y_in, z, nw)


def _hgrn_kernel(rev, sub, *refs):
    if rev:
        (q_ref, f_ref, i_ref, lbf_ref, om_ref, tri_ref, yin_ref, g_ref, nw_ref,
         o_ref, st_ref) = refs
    else:
        (q_ref, f_ref, i_ref, lbf_ref, om_ref, tri_ref, o_ref, st_ref) = refs
    c = pl.program_id(1)
    n = HGRN_CHUNK

    @pl.when(c == 0)
    def _():
        st_ref[...] = jnp.zeros_like(st_ref)

    li = lax.broadcasted_iota(jnp.int32, (n, n), 0)
    si = lax.broadcasted_iota(jnp.int32, (n, n), 1)
    level_masks = []
    m = HGRN_DIAG
    while 2 * m <= n:
        sh = m.bit_length() - 1
        same_pair = (li >> (sh + 1)) == (si >> (sh + 1))
        q_half = ((li >> sh) & 1) == (0 if rev else 1)
        k_half = ((si >> sh) & 1) == (1 if rev else 0)
        level_masks.append((m, same_pair & q_half & k_half))
        m *= 2
    rowi = lax.broadcasted_iota(jnp.int32, (HGRN_DIAG, LANE), 0)

    def chunk(j, carry):
        r0 = pl.multiple_of(((sub - 1 - j) if rev else j) * n, n)
        rows = pl.ds(r0, n)
        raw = f_ref[rows, :].astype(F32)
        t = jnp.exp(-jnp.abs(raw))
        r = 1.0 / (1.0 + t)
        nonneg = raw >= 0.0
        sig = jnp.where(nonneg, r, t * r)
        sig_neg = jnp.where(nonneg, t * r, r)
        om = om_ref[...]
        logf = jnp.log(lbf_ref[...] + om * sig)
        kk = om * sig_neg
        u = _split_dot(tri_ref[...], logf * LOG2E)
        qf = q_ref[rows, :].astype(F32) * (H_KEY ** -0.5)
        v = i_ref[rows, :].astype(F32)

        u_end = u[0:1] if rev else u[n - 1:n]
        qb = (qf * jnp.exp2(u)).astype(BF16)
        kw = (kk * jnp.exp2(u_end - u)).astype(BF16)
        cd = jnp.exp2(u_end)

        levels = []
        for m, msk in level_masks:
            refs_rows = []
            for p in range(n // (2 * m)):
                ridx = 2 * m * p + (m if rev else m - 1)
                refs_rows.append(jnp.broadcast_to(u[ridx:ridx + 1], (2 * m, H_WIDTH)))
            rb = refs_rows[0] if len(refs_rows) == 1 else jnp.concatenate(refs_rows, axis=0)
            qt = (qf * jnp.exp2(u - rb)).astype(BF16)
            kt = (kk * jnp.exp2(rb - u)).astype(BF16)
            levels.append((qt, kt, msk))

        def diag_blocks(h):
            sl = slice(h * LANE, (h + 1) * LANE)
            pieces = []
            for b in range(n // HGRN_DIAG):
                b0 = b * HGRN_DIAG
                q_blk = qf[b0:b0 + HGRN_DIAG, sl]
                u_blk = u[b0:b0 + HGRN_DIAG, sl]
                o_blk = None
                for s in range(HGRN_DIAG):
                    k_row = kk[b0 + s:b0 + s + 1, sl]
                    u_row = u[b0 + s:b0 + s + 1, sl]
                    v_row = v[b0 + s:b0 + s + 1, sl]
                    w = q_blk * (k_row * jnp.exp2(u_blk - u_row))
                    att = jnp.sum(w, axis=-1, keepdims=True)
                    valid = (rowi <= s) if rev else (rowi >= s)
                    term = jnp.where(valid, att, 0.0) * v_row
                    o_blk = term if o_blk is None else o_blk + term
                pieces.append(o_blk)
            return jnp.concatenate(pieces, axis=0)

        for h in range(H_HEADS):
            sl = slice(h * LANE, (h + 1) * LANE)
            att = None
            for qt, kt, msk in levels:
                part = jnp.where(msk, _dot_nt(qt[:, sl], kt[:, sl]), 0.0)
                att = part if att is None else att + part
            st = st_ref[h]
            v_t = v[:, sl].T.astype(BF16)
            lhs = jnp.concatenate([att.astype(BF16), qb[:, sl]], axis=1)
            rhs = jnp.concatenate([v_t, st.astype(BF16)], axis=1)
            oh = _dot_nt(lhs, rhs) + diag_blocks(h)
            st_ref[h] = st * cd[:, sl] + _dot(v_t, kw[:, sl])
            if not rev:
                o_ref[rows, sl] = oh
            else:
                oh = oh + yin_ref[rows, sl]
                ms = jnp.mean(oh * oh, axis=-1, keepdims=True)
                gf = g_ref[rows, sl].astype(F32)
                o_ref[rows, sl] = (oh * lax.rsqrt(ms + EPS) * nw_ref[...]
                                   * (gf * _sigmoid(gf))).astype(o_ref.dtype)
        return carry

    lax.fori_loop(0, sub, chunk, 0)


def _hgrn(rev, nb, seq, hin, lbf, om, tri, extra):
    sub = _chunks_per_step(seq // HGRN_CHUNK, CHUNKS_PER_STEP)
    n = sub * HGRN_CHUNK
    nc = seq // n
    t = nb * seq

    def blk(col):
        return lambda b, c: (b * nc + (nc - 1 - c if rev else c), col)

    const = lambda b, c: (0, 0)
    in_specs = [
        pl.BlockSpec((n, H_WIDTH), blk(0)),
        pl.BlockSpec((n, H_WIDTH), blk(2 if rev else 1)),
        pl.BlockSpec((n, H_WIDTH), blk(3)),
        pl.BlockSpec((1, H_WIDTH), const),
        pl.BlockSpec((1, H_WIDTH), const),
        pl.BlockSpec((HGRN_CHUNK, HGRN_CHUNK), const),
    ]
    args = [hin, hin, hin, lbf, om, tri]
    if rev:
        y_in, nw = extra
        in_specs += [pl.BlockSpec((n, H_WIDTH), blk(0)), pl.BlockSpec((n, H_WIDTH), blk(4)),
                     pl.BlockSpec((1, LANE), const)]
        args += [y_in, hin, nw]
        out_dtype = BF16
    else:
        out_dtype = F32
    return pl.pallas_call(
        functools.partial(_hgrn_kernel, rev, sub),
        grid=(nb, nc),
        in_specs=in_specs,
        out_specs=pl.BlockSpec((n, H_WIDTH), blk(0)),
        out_shape=jax.ShapeDtypeStruct((t, H_WIDTH), out_dtype),
        scratch_shapes=[pltpu.VMEM((H_HEADS, LANE, LANE), F32)],
        compiler_params=_params(dimension_semantics=("arbitrary", "arbitrary")),
        name="hgrn_rev" if rev else "hgrn_fwd",
    )(*args)


def _attn_kernel(tq, tk, bounded_ref, q_ref, k_ref, v_ref, o_ref, m_s, acc_s):
    nk = k_ref.shape[0] // tk
    lane = lax.broadcasted_iota(jnp.int32, acc_s.shape, 1)
    for g in range(A_KV):
        gl = slice(g * LANE, (g + 1) * LANE)
        qs = jnp.concatenate(
            [q_ref[:, (g * A_GROUP + r) * LANE:(g * A_GROUP + r + 1) * LANE] for r in range(A_GROUP)], axis=0)
        acc_s[...] = jnp.zeros_like(acc_s)

        def tiles(kt):
            k0 = pl.multiple_of(kt * tk, tk)
            return k_ref[pl.ds(k0, tk), gl], v_ref[pl.ds(k0, tk), gl]

        def shifted_body(kt, carry):
            kb, vb = tiles(kt)
            acc_s[...] += _dot(jnp.exp2(_dot_nt(qs, kb)).astype(BF16), vb)
            return carry

        def online_body(kt, carry):
            kb, vb = tiles(kt)
            s = _dot_nt(qs, kb)
            m_prev = m_s[...]
            m_new = jnp.maximum(m_prev, jnp.max(s, axis=-1, keepdims=True))
            p = jnp.exp2(s - m_new[:, 0:1])
            acc_s[...] = jnp.exp2(m_prev - m_new) * acc_s[...] + _dot(p.astype(BF16), vb)
            m_s[...] = m_new
            return carry

        @pl.when(bounded_ref[0] != 0)
        def _():
            lax.fori_loop(0, nk, shifted_body, 0, unroll=min(4, nk))

        @pl.when(bounded_ref[0] == 0)
        def _():
            m_s[...] = jnp.full_like(m_s, -jnp.inf)
            lax.fori_loop(0, nk, online_body, 0)

        o = acc_s[...]
        o = jnp.where(lane < A_HD, o / o[:, A_HD:A_HD + 1], 0.0)
        heads = [o[r * tq:(r + 1) * tq] for r in range(A_GROUP)]
        for pr in range(A_GROUP // 2):
            packed = heads[2 * pr] + pltpu.roll(heads[2 * pr + 1], A_HD, 1)
            c0 = g * A_GROUP * A_HD + pr * LANE
            o_ref[:, c0:c0 + LANE] = packed.astype(o_ref.dtype)


def _attn(nb, seq, bounded, qh, kh, vh, tq, tk):
    t = nb * seq
    nq = seq // tq
    rows = A_GROUP * tq
    return pl.pallas_call(
        functools.partial(_attn_kernel, tq, tk),
        grid=(nb, nq),
        in_specs=[
            pl.BlockSpec(memory_space=pltpu.SMEM),
            pl.BlockSpec((tq, A_HEADS * LANE), lambda b, i: (b * nq + i, 0)),
            pl.BlockSpec((seq, A_KV * LANE), lambda b, i: (b, 0)),
            pl.BlockSpec((seq, A_KV * LANE), lambda b, i: (b, 0)),
        ],
        out_specs=pl.BlockSpec((tq, A_HEADS * A_HD), lambda b, i: (b * nq + i, 0)),
        out_shape=jax.ShapeDtypeStruct((t, A_HEADS * A_HD), BF16),
        scratch_shapes=[pltpu.VMEM((rows, LANE), F32) for _ in range(2)],
        compiler_params=_params(dimension_semantics=("parallel", "parallel")),
        name="attn",
    )(bounded, qh, kh, vh)


def _merge_kernel(x_ref, ym_ref, yh_ref, ya_ref, g_ref, wb_ref, wo_ref, nw_ref, o_ref):
    mixed = None
    for i, y_ref in enumerate((ym_ref, yh_ref, ya_ref)):
        proj = _dot(y_ref[...], wb_ref[i])
        gate = _sigmoid(g_ref[:, i * D_MODEL:(i + 1) * D_MODEL].astype(F32))
        mixed = gate * proj if mixed is None else mixed + gate * proj
    out = _dot(mixed.astype(BF16), wo_ref[...])
    o_ref[...] = x_ref[...] + _rms(out, nw_ref[...])


def _merge(x, ym, yh, ya, gates, wb, wo, nw, tm):
    t = x.shape[0]
    row = lambda i: (i, 0)
    return pl.pallas_call(
        _merge_kernel,
        grid=(t // tm,),
        in_specs=[
            pl.BlockSpec((tm, D_MODEL), row),
            pl.BlockSpec((tm, M_INNER), row),
            pl.BlockSpec((tm, H_WIDTH), row),
            pl.BlockSpec((tm, A_HEADS * A_HD), row),
            pl.BlockSpec((tm, 3 * D_MODEL), row),
            pl.BlockSpec((3, M_INNER, D_MODEL), lambda i: (0, 0, 0)),
            pl.BlockSpec((D_MODEL, D_MODEL), lambda i: (0, 0)),
            pl.BlockSpec((1, D_MODEL), lambda i: (0, 0)),
        ],
        out_specs=pl.BlockSpec((tm, D_MODEL), row),
        out_shape=jax.ShapeDtypeStruct((t, D_MODEL), F32),
        compiler_params=_params(dimension_semantics=("parallel",)),
        name="merge",
    )(x, ym, yh, ya, gates, wb, wo, nw)


def _ffn_kernel(tiles_per_seq, xp_ref, xm_ref, xn_ref, nw1_ref, wup_ref, cw_ref, cb_ref, wdn_ref,
                nw2_ref, o_ref, u_s):
    j = pl.program_id(0) % tiles_per_seq
    tm = xm_ref.shape[0]
    x = xm_ref[...]
    nw1 = nw1_ref[...]
    hp = jnp.where(j > 0, _rms(xp_ref[...], nw1), 0.0)
    hn = jnp.where(j < tiles_per_seq - 1, _rms(xn_ref[...], nw1), 0.0)
    hext = jnp.concatenate([hp, _rms(x, nw1), hn], axis=0).astype(BF16)

    def up(slot, c0):
        u_s[slot] = _dot(hext, wup_ref[:, c0:c0 + FFN_COLS])

    def conv(slot, c0):
        w = cw_ref[:, c0:c0 + FFN_COLS]
        out = cb_ref[:, c0:c0 + FFN_COLS]
        for k in range(3):
            out = out + u_s[slot, pl.ds(FFN_HALO - 1 + k, tm), :] * w[k:k + 1]
        return out

    n_chunks = D_FF // FFN_COLS
    acc = jnp.zeros((tm, D_MODEL), F32)
    up(0, 0)
    up(1, D_FF)
    for cc in range(n_chunks):
        c0 = cc * FFN_COLS
        par = 2 * (cc % 2)
        if cc + 1 < n_chunks:
            up(2 - par, c0 + FFN_COLS)
            up(3 - par, D_FF + c0 + FFN_COLS)
        ug = conv(par, c0)
        uu = conv(par + 1, D_FF + c0)
        act = (ug * _sigmoid(ug) * uu).astype(BF16)
        acc = acc + _dot(act, wdn_ref[c0:c0 + FFN_COLS, :])
    o_ref[...] = x + _rms(acc, nw2_ref[...])


def _ffn(x, nw1, wup, cw, cb, wdn, nw2, seq, tm):
    t = x.shape[0]
    tiles_per_seq = seq // tm
    hb = tm // FFN_HALO
    last_halo = t // FFN_HALO - 1
    const = lambda i: (0, 0)
    single = pl.Buffered(1)
    return pl.pallas_call(
        functools.partial(_ffn_kernel, tiles_per_seq),
        grid=(t // tm,),
        in_specs=[
            pl.BlockSpec((FFN_HALO, D_MODEL), lambda i: (jnp.maximum(i * hb - 1, 0), 0)),
            pl.BlockSpec((tm, D_MODEL), lambda i: (i, 0)),
            pl.BlockSpec((FFN_HALO, D_MODEL), lambda i: (jnp.minimum(i * hb + hb, last_halo), 0)),
            pl.BlockSpec((1, D_MODEL), const),
            pl.BlockSpec((D_MODEL, 2 * D_FF), const, pipeline_mode=single),
            pl.BlockSpec((3, 2 * D_FF), const),
            pl.BlockSpec((1, 2 * D_FF), const),
            pl.BlockSpec((D_FF, D_MODEL), const, pipeline_mode=single),
            pl.BlockSpec((1, D_MODEL), const),
        ],
        out_specs=pl.BlockSpec((tm, D_MODEL), lambda i: (i, 0)),
        out_shape=jax.ShapeDtypeStruct((t, D_MODEL), F32),
        scratch_shapes=[pltpu.VMEM((4, tm + 2 * FFN_HALO, FFN_COLS), F32)],
        compiler_params=_params(dimension_semantics=("parallel",)),
        name="ffn",
    )(x, x, x, nw1, wup, cw, cb, wdn, nw2)


def _pack_w_in(w):
    sizes = (M_INNER, M_XBC, M_HEADS, M_HEADS, H_WIDTH, H_WIDTH, H_WIDTH, H_WIDTH, H_WIDTH,
             A_HEADS * A_HD, A_KV * A_HD, A_KV * A_HD, 3 * D_MODEL)
    parts, off = [], 0
    for s in sizes:
        parts.append(w[:, off:off + s])
        off += s
    (m_z, m_xbc, m_dtf, m_dtb, h_q, h_ff, h_fb, h_i, h_g, a_q, a_k, a_v, gate) = parts
    dt = jnp.pad(jnp.concatenate([m_dtf, m_dtb], axis=1), ((0, 0), (0, LANE - 2 * M_HEADS)))
    packed = jnp.concatenate(
        [m_z, m_xbc, dt, h_q, h_ff, h_fb, h_i, h_g, a_q, a_k, a_v, gate], axis=1)
    return packed.astype(BF16)


def _pad_lane(v):
    v = v.reshape(1, -1)
    return jnp.pad(v, ((0, 0), (0, LANE - v.shape[1])))


def _rope_tables(n):
    pos = jnp.arange(n)
    row = (pos // GRID_W).astype(F32)
    col = (pos % GRID_W).astype(F32)
    half = A_HD // 4
    inv = ROPE_BASE ** (-jnp.arange(0, 2 * half, 2, dtype=F32) / (2 * half))
    ar = row[:, None] * inv
    ac = col[:, None] * inv
    cr, sr, cc, sc = jnp.cos(ar), jnp.sin(ar), jnp.cos(ac), jnp.sin(ac)
    cos = jnp.concatenate([cr, cr, cc, cc] * (LANE // A_HD), axis=1)
    sin = jnp.concatenate([-sr, sr, -sc, sc] * (LANE // A_HD), axis=1)
    return cos, sin


def _tri(n, rev):
    i = jnp.arange(n)
    m = (i[None, :] >= i[:, None]) if rev else (i[None, :] <= i[:, None])
    return m.astype(BF16)


def _pick_tile(seq, want):
    return want if seq % want == 0 else seq


def kernel(x_prompt, x_sample, norm_mix_pre, w_in, m_conv_w, m_conv_b, m_dt_bias, m_a_log, m_d, m_norm_w, h_lb_logits, h_norm_w, a_q_norm, a_k_norm, w_branch, w_out, norm_mix_post, norm_ffn_pre, f_w_up, f_conv_w, f_conv_b, f_w_down, norm_ffn_post):
    sm = jax.nn.softmax(h_lb_logits.astype(F32), axis=0)
    tail = jnp.concatenate([jnp.zeros_like(sm[:1]), sm[1:]], axis=0)
    lower = jnp.cumsum(tail, axis=0)
    one_minus = sm[0:1] + (jnp.sum(tail, axis=0, keepdims=True) - lower)
    lower_floor = jnp.maximum(lower, LB_FLOOR)

    lane_a = jnp.arange(LANE) == A_HD
    kadd = lane_a.astype(F32).reshape(1, LANE)
    layers = []
    for l in range(DEPTH):
        score_bound = A_HD ** 0.5 * jnp.max(jnp.abs(a_q_norm[l])) * jnp.max(jnp.abs(a_k_norm[l]))
        bounded = score_bound <= ATT_SHIFT_LIMIT
        shift = jnp.where(bounded, score_bound * LOG2E, 0.0)
        layers.append(dict(
            bounded=bounded.astype(jnp.int32).reshape(1),
            qadd=jnp.where(lane_a, -shift, 0.0).astype(F32).reshape(1, LANE), kadd=kadd,
            nw_pre=norm_mix_pre[l].reshape(1, D_MODEL),
            w_in=_pack_w_in(w_in[l]),
            cw=m_conv_w[l], cb=m_conv_b[l].reshape(1, M_XBC),
            dtb=_pad_lane(m_dt_bias[l]), a_neg=_pad_lane(-jnp.exp(m_a_log[l].astype(F32)) * LOG2E),
            dskip=jnp.repeat(m_d[l], M_INNER // M_HEADS).reshape(1, M_INNER),
            m_nw=m_norm_w[l].reshape(1, M_INNER),
            lbf=lower_floor[l], om=one_minus[l],
            h_nw=h_norm_w[l].reshape(1, LANE),
            qw=jnp.tile(a_q_norm[l] * (A_HD ** -0.5 * LOG2E), LANE // A_HD).reshape(1, LANE),
            kw=jnp.tile(a_k_norm[l], LANE // A_HD).reshape(1, LANE),
            wb=w_branch[l].astype(BF16), wo=w_out[l].astype(BF16),
            nw_post=norm_mix_post[l].reshape(1, D_MODEL),
            nw_ffn=norm_ffn_pre[l].reshape(1, D_MODEL),
            wup=f_w_up[l].astype(BF16), fcw=f_conv_w[l], fcb=f_conv_b[l].reshape(1, 2 * D_FF),
            wdn=f_w_down[l].astype(BF16),
            nw_ffn_post=norm_ffn_post[l].reshape(1, D_MODEL),
        ))
    tri_ssd = (_tri(SSD_CHUNK, False), _tri(SSD_CHUNK, True))
    tri_hgrn = (_tri(HGRN_CHUNK, False), _tri(HGRN_CHUNK, True))
    head_of_lane = jnp.arange(M_INNER) // (M_INNER // M_HEADS)
    expand_ssd = tuple(
        (jnp.arange(LANE)[:, None] == d * M_HEADS + head_of_lane[None, :]).astype(BF16) for d in range(2))

    def run(x):
        nb, seq, _ = x.shape
        xf = x.reshape(nb * seq, D_MODEL)
        cos, sin = _rope_tables(seq)
        tm = _pick_tile(seq, 512)
        tq = _pick_tile(seq, 256)
        tk = _pick_tile(seq, 512)
        for p in layers:
            z, xbc, dtraw, hin, qh, kh, vh, gates = _inproj(
                xf, p["nw_pre"], p["w_in"], cos, sin, p["qw"], p["kw"],
                p["qadd"], p["kadd"], seq, tm)
            y_f, xc = _ssd_fwd(nb, seq, xbc, dtraw, p["cw"], p["cb"], p["dtb"], p["a_neg"],
                               tri_ssd[0], expand_ssd[0], p["dskip"])
            y_m = _ssd_rev(nb, seq, xc, dtraw, p["dtb"], p["a_neg"], tri_ssd[1], expand_ssd[1],
                           y_f, z, p["m_nw"])
            o_f = _hgrn(False, nb, seq, hin, p["lbf"][0:1], p["om"][0:1], tri_hgrn[0], None)
            y_h = _hgrn(True, nb, seq, hin, p["lbf"][1:2], p["om"][1:2], tri_hgrn[1], (o_f, p["h_nw"]))
            y_a = _attn(nb, seq, p["bounded"], qh, kh, vh, tq, tk)
            xf = _merge(xf, y_m, y_h, y_a, gates, p["wb"], p["wo"], p["nw_post"], tm)
            xf = _ffn(xf, p["nw_ffn"], p["wup"], p["fcw"], p["fcb"], p["wdn"], p["nw_ffn_post"], seq, tm)
        return xf.reshape(nb, seq, D_MODEL)

    return (run(x_prompt), run(x_sample))
```

```python
import functools

import jax
import jax.numpy as jnp
from jax import lax
from jax.experimental import pallas as pl
from jax.experimental.pallas import tpu as pltpu

F32 = jnp.float32
BF16 = jnp.bfloat16

D_MODEL = 1024
DEPTH = 4
EPS = 1e-6
LB_FLOOR = 1e-30
GRID_W = 64
ROPE_BASE = 10000.0

M_HEADS = 8
M_INNER = 512
M_STATE = 64
M_CONV = 4
M_XBC = 768
SSD_CHUNK = 128
CHUNKS_PER_STEP = 4

H_HEADS = 4
H_KEY = 128
H_WIDTH = 512
HGRN_CHUNK = 128
HGRN_DIAG = 8

A_HEADS = 8
A_KV = 2
A_HD = 64
A_GROUP = A_HEADS // A_KV

D_FF = 2816
FFN_COLS = 256
FFN_DOWN_GROUP = 4
FFN_HALO = 8
XBC_HALO = 16

LANE = 128
LOG2E = 1.4426950408889634
ATT_SHIFT_LIMIT = 40.0
VMEM_LIMIT = 56 * 1024 * 1024

OFF_Z = 0
OFF_XBC = OFF_Z + M_INNER
OFF_DT = OFF_XBC + M_XBC
OFF_H = OFF_DT + LANE
OFF_Q = OFF_H + 5 * H_WIDTH
OFF_K = OFF_Q + A_HEADS * A_HD
OFF_V = OFF_K + A_KV * A_HD
OFF_G = OFF_V + A_KV * A_HD
N_PACKED = OFF_G + 3 * D_MODEL


def _rms(x, w):
    return x * lax.rsqrt(jnp.mean(x * x, axis=-1, keepdims=True) + EPS) * w


def _sigmoid(x):
    return 1.0 / (1.0 + jnp.exp(-x))


def _dot(a, b):
    return jnp.dot(a, b, preferred_element_type=F32)


def _dot_nt(a, b):
    return lax.dot_general(a, b, (((1,), (1,)), ((), ())), preferred_element_type=F32)


def _split_dot(tri, x):
    hi = x.astype(BF16)
    lo = (x - hi.astype(F32)).astype(BF16)
    return _dot(tri, hi) + _dot(tri, lo)


def _params(**kw):
    return pltpu.CompilerParams(vmem_limit_bytes=VMEM_LIMIT, **kw)


def _inproj_kernel(x_ref, nw_ref, w_ref, cos_ref, sin_ref, qw_ref, kw_ref, qadd_ref, kadd_ref,
                   oz, oxbc, odt, oh, oq, ok, ovt, og):
    h = _rms(x_ref[...], nw_ref[...]).astype(BF16)

    def mm(c0, n):
        return _dot(h, w_ref[:, c0:c0 + n])

    def plain(o_ref, c0, width, step):
        for j in range(0, width, step):
            o_ref[:, j:j + step] = mm(c0 + j, step).astype(o_ref.dtype)

    plain(oz, OFF_Z, M_INNER, 512)
    plain(oxbc, OFF_XBC, M_XBC, 256)
    odt[...] = mm(OFF_DT, LANE)
    plain(oh, OFF_H, 5 * H_WIDTH, 512)

    cos = cos_ref[...]
    sin = sin_ref[...]
    lane = lax.broadcasted_iota(jnp.int32, cos.shape, 1)
    first_half = (lane & 16) == 0
    low = lane < A_HD

    def split_pair(y, add):
        first = jnp.where(low, y, 0.0) + add
        second = jnp.where(low, pltpu.roll(y, A_HD, 1), 0.0) + add
        return first, second

    def norm_rope(y, w):
        sq = y * y
        ss_lo = jnp.sum(jnp.where(low, sq, 0.0), axis=-1, keepdims=True)
        ss_hi = jnp.sum(jnp.where(low, 0.0, sq), axis=-1, keepdims=True)
        yn = y * lax.rsqrt(jnp.where(low, ss_lo, ss_hi) * (1.0 / A_HD) + EPS) * w
        partner = jnp.where(first_half, pltpu.roll(yn, LANE - 16, 1), pltpu.roll(yn, 16, 1))
        return yn * cos + partner * sin

    qw = qw_ref[...]
    qadd = qadd_ref[...]
    kv_add = kadd_ref[...]
    yq = mm(OFF_Q, A_HEADS * A_HD)
    for pr in range(A_HEADS // 2):
        a, b = split_pair(norm_rope(yq[:, pr * LANE:(pr + 1) * LANE], qw), qadd)
        oq[:, 2 * pr * LANE:(2 * pr + 1) * LANE] = a.astype(BF16)
        oq[:, (2 * pr + 1) * LANE:(2 * pr + 2) * LANE] = b.astype(BF16)
    ykv = mm(OFF_K, 2 * A_KV * A_HD)
    a, b = split_pair(norm_rope(ykv[:, :LANE], kw_ref[...]), kv_add)
    ok[:, :LANE] = a.astype(BF16)
    ok[:, LANE:] = b.astype(BF16)
    a, b = split_pair(ykv[:, LANE:], kv_add)
    ovt[:LANE, :] = a.T.astype(BF16)
    ovt[LANE:, :] = b.T.astype(BF16)
    plain(og, OFF_G, 3 * D_MODEL, 512)


def _inproj(x, nw, w, cos, sin, qw, kw, qadd, kadd, seq, tm):
    t = x.shape[0]
    tiles_per_seq = seq // tm
    row = lambda i: (i, 0)
    const = lambda i: (0, 0)
    pos = lambda i: (i % tiles_per_seq, 0)
    outs = ((M_INNER, BF16), (M_XBC, BF16), (LANE, F32), (5 * H_WIDTH, BF16), (A_HEADS * LANE, BF16),
            (A_KV * LANE, BF16), None, (3 * D_MODEL, BF16))
    out_specs = [pl.BlockSpec((A_KV * LANE, tm), lambda i: (0, i)) if o is None
                 else pl.BlockSpec((tm, o[0]), row) for o in outs]
    out_shape = [jax.ShapeDtypeStruct((A_KV * LANE, t), BF16) if o is None
                 else jax.ShapeDtypeStruct((t, o[0]), o[1]) for o in outs]
    return pl.pallas_call(
        _inproj_kernel,
        grid=(t // tm,),
        in_specs=[
            pl.BlockSpec((tm, D_MODEL), row),
            pl.BlockSpec((1, D_MODEL), const),
            pl.BlockSpec((D_MODEL, N_PACKED), const, pipeline_mode=pl.Buffered(1)),
            pl.BlockSpec((tm, LANE), pos),
            pl.BlockSpec((tm, LANE), pos),
            pl.BlockSpec((1, LANE), const),
            pl.BlockSpec((1, LANE), const),
            pl.BlockSpec((1, LANE), const),
            pl.BlockSpec((1, LANE), const),
        ],
        out_specs=out_specs,
        out_shape=out_shape,
        compiler_params=_params(dimension_semantics=("parallel",)),
        name="inproj",
    )(x, nw, w, cos, sin, qw, kw, qadd, kadd)


def _ssd_kernel(rev, nblk, sub, *refs):
    if rev:
        (xc_ref, dt_ref, dtb_ref, a_ref, tri_ref, e_ref, yin_ref, z_ref, nw_ref, o_ref, s_ref) = refs
    else:
        (xp_ref, xm_ref, xn_ref, dt_ref, cw_ref, cb_ref, dtb_ref, a_ref, tri_ref, e_ref,
         dsk_ref, o_ref, oxc_ref, s_ref, xc_s) = refs
    c = pl.program_id(1)
    q = SSD_CHUNK

    @pl.when(c == 0)
    def _():
        s_ref[...] = jnp.zeros_like(s_ref)

    if not rev:
        xp = jnp.where(c > 0, xp_ref[...].astype(F32), 0.0)
        xn = jnp.where(c < nblk - 1, xn_ref[...].astype(F32), 0.0)
        xe = jnp.concatenate([xp, xm_ref[...].astype(F32), xn], axis=0)
        cw = cw_ref[...]
        y = cb_ref[...]
        for k in range(M_CONV):
            start = XBC_HALO - M_CONV // 2 + k
            y = y + xe[start:start + sub * q] * cw[k:k + 1]
        xc_all = y * _sigmoid(y)
        xc_s[...] = xc_all
        oxc_ref[...] = xc_all.astype(oxc_ref.dtype)

    li = lax.broadcasted_iota(jnp.int32, (q, q), 0)
    si = lax.broadcasted_iota(jnp.int32, (q, q), 1)
    mask = (si >= li) if rev else (si <= li)
    lane_s = lax.broadcasted_iota(jnp.int32, (q, LANE), 1)
    lane_x = lax.broadcasted_iota(jnp.int32, (q, 2 * LANE), 1)
    base = M_HEADS if rev else 0
    hpg = M_HEADS // 2

    def chunk(j, carry):
        r0 = pl.multiple_of(((sub - 1 - j) if rev else j) * q, q)
        rows = pl.ds(r0, q)
        xc = xc_ref[rows, :].astype(F32) if rev else xc_s[rows, :]
        xs = xc[:, :M_INNER]
        bm = xc[:, M_INNER:M_INNER + LANE]
        cm = xc[:, M_INNER + LANE:]

        dtr = dt_ref[rows, :] + dtb_ref[...]
        dt = jnp.maximum(dtr, 0.0) + jnp.log1p(jnp.exp(-jnp.abs(dtr)))
        a = dt * a_ref[...]
        u = _split_dot(tri_ref[...], a)
        wide = _expand_heads(jnp.concatenate([u, dt], axis=0), e_ref[...])
        u_w = wide[:q]
        dt_w = wide[q:]
        u_end_w = u_w[0:1] if rev else u_w[q - 1:q]
        off_w = jnp.exp2(u_w)
        w_state_w = dt_w * jnp.exp2(u_end_w - u_w)
        cd_w = jnp.exp2(u_end_w)
        u_t = u.T
        dt_t = dt.T

        ys = []
        for g in range(2):
            sl = slice(g * 2 * LANE, (g + 1) * 2 * LANE)
            gsel = (lane_s >= M_STATE) if g else (lane_s < M_STATE)
            cg = jnp.where(gsel, cm, 0.0).astype(BF16)
            bg = jnp.where(gsel, bm, 0.0).astype(BF16)
            cbm = _dot_nt(cg, bm.astype(BF16))
            xg = xs[:, sl]
            lhs = []
            for r in range(hpg):
                ln = base + g * hpg + r
                dec = jnp.exp2(jnp.where(mask, u[:, ln:ln + 1] - u_t[ln:ln + 1, :], -1e30))
                lhs.append((cbm * dec * dt_t[ln:ln + 1, :]).astype(BF16))
            xr = [jnp.where((lane_x >> 6) == r, xg, 0.0).astype(BF16) for r in range(hpg)]
            y_diag = _dot(jnp.concatenate(lhs, axis=1), jnp.concatenate(xr, axis=0))
            sg = s_ref[g]
            ys.append(y_diag + _dot(cg, sg.astype(BF16)) * off_w[:, sl])
            st = lax.dot_general(bg, (xg * w_state_w[:, sl]).astype(BF16),
                                 (((0,), (0,)), ((), ())), preferred_element_type=F32)
            s_ref[g] = sg * cd_w[:, sl] + st
        yc = jnp.concatenate(ys, axis=1)
        if not rev:
            o_ref[rows, :] = yc + xs * dsk_ref[...]
        else:
            yt = yin_ref[rows, :] + yc
            zf = z_ref[rows, :].astype(F32)
            yt = yt * (zf * _sigmoid(zf))
            nw = nw_ref[...]
            for g in range(2):
                sl = slice(g * 2 * LANE, (g + 1) * 2 * LANE)
                blk = yt[:, sl]
                ms = jnp.mean(blk * blk, axis=-1, keepdims=True)
                o_ref[rows, sl] = (blk * lax.rsqrt(ms + EPS) * nw[:, sl]).astype(o_ref.dtype)
        return carry

    lax.fori_loop(0, sub, chunk, 0, unroll=True)


def _expand_heads(x, e):
    hi = x.astype(BF16)
    rest = x - hi.astype(F32)
    mid = rest.astype(BF16)
    lo = (rest - mid.astype(F32)).astype(BF16)
    return _dot(hi, e) + _dot(mid, e) + _dot(lo, e)


def _chunks_per_step(n_chunks, want):
    while n_chunks % want:
        want -= 1
    return want


def _ssd_fwd(nb, seq, xbc, dtraw, cw, cb, dtb, a_neg, tri, expand, dskip):
    q = SSD_CHUNK
    sub = _chunks_per_step(seq // q, CHUNKS_PER_STEP)
    rows = sub * q
    nblk = seq // rows
    t = nb * seq
    hb = rows // XBC_HALO
    last_halo = t // XBC_HALO - 1
    main = lambda b, c: (b * nblk + c, 0)
    prev = lambda b, c: (jnp.maximum((b * nblk + c) * hb - 1, 0), 0)
    nxt = lambda b, c: (jnp.minimum((b * nblk + c) * hb + hb, last_halo), 0)
    const = lambda b, c: (0, 0)
    return pl.pallas_call(
        functools.partial(_ssd_kernel, False, nblk, sub),
        grid=(nb, nblk),
        in_specs=[
            pl.BlockSpec((XBC_HALO, M_XBC), prev),
            pl.BlockSpec((rows, M_XBC), main),
            pl.BlockSpec((XBC_HALO, M_XBC), nxt),
            pl.BlockSpec((rows, LANE), main),
            pl.BlockSpec((M_CONV, M_XBC), const),
            pl.BlockSpec((1, M_XBC), const),
            pl.BlockSpec((1, LANE), const),
            pl.BlockSpec((1, LANE), const),
            pl.BlockSpec((q, q), const),
            pl.BlockSpec((LANE, M_INNER), const),
            pl.BlockSpec((1, M_INNER), const),
        ],
        out_specs=[pl.BlockSpec((rows, M_INNER), main), pl.BlockSpec((rows, M_XBC), main)],
        out_shape=[jax.ShapeDtypeStruct((t, M_INNER), F32), jax.ShapeDtypeStruct((t, M_XBC), BF16)],
        scratch_shapes=[pltpu.VMEM((2, LANE, 2 * LANE), F32), pltpu.VMEM((rows, M_XBC), F32)],
        compiler_params=_params(dimension_semantics=("arbitrary", "arbitrary")),
        name="ssd_fwd",
    )(xbc, xbc, xbc, dtraw, cw, cb, dtb, a_neg, tri, expand, dskip)


def _ssd_rev(nb, seq, xc, dtraw, dtb, a_neg, tri, expand, y_in, z, nw):
    q = SSD_CHUNK
    sub = _chunks_per_step(seq // q, CHUNKS_PER_STEP)
    rows = sub * q
    nblk = seq // rows
    t = nb * seq
    main = lambda b, c: (b * nblk + nblk - 1 - c, 0)
    const = lambda b, c: (0, 0)
    return pl.pallas_call(
        functools.partial(_ssd_kernel, True, nblk, sub),
        grid=(nb, nblk),
        in_specs=[
            pl.BlockSpec((rows, M_XBC), main),
            pl.BlockSpec((rows, LANE), main),
            pl.BlockSpec((1, LANE), const),
            pl.BlockSpec((1, LANE), const),
            pl.BlockSpec((q, q), const),
            pl.BlockSpec((LANE, M_INNER), const),
            pl.BlockSpec((rows, M_INNER), main),
            pl.BlockSpec((rows, M_INNER), main),
            pl.BlockSpec((1, M_INNER), const),
        ],
        out_specs=pl.BlockSpec((rows, M_INNER), main),
        out_shape=jax.ShapeDtypeStruct((t, M_INNER), BF16),
        scratch_shapes=[pltpu.VMEM((2, LANE, 2 * LANE), F32)],
        compiler_params=_params(dimension_semantics=("arbitrary", "arbitrary")),
        name="ssd_rev",
    )(xc, dtraw, dtb, a_neg, tri, expand, y_in, z, nw)


def _hgrn_kernel(rev, sub, *refs):
    if rev:
        (q_ref, f_ref, i_ref, lbf_ref, om_ref, tri_ref, yin_ref, g_ref, nw_ref,
         o_ref, st_ref, k_s, u_s) = refs
    else:
        (q_ref, f_ref, i_ref, lbf_ref, om_ref, tri_ref, o_ref, st_ref, k_s, u_s) = refs
    c = pl.program_id(1)
    n = HGRN_CHUNK

    @pl.when(c == 0)
    def _():
        st_ref[...] = jnp.zeros_like(st_ref)

    li = lax.broadcasted_iota(jnp.int32, (n, n), 0)
    si = lax.broadcasted_iota(jnp.int32, (n, n), 1)
    level_masks = []
    m = HGRN_DIAG
    while 2 * m <= n:
        sh = m.bit_length() - 1
        same_pair = (li >> (sh + 1)) == (si >> (sh + 1))
        q_half = ((li >> sh) & 1) == (0 if rev else 1)
        k_half = ((si >> sh) & 1) == (1 if rev else 0)
        level_masks.append((m, same_pair & q_half & k_half))
        m *= 2
    rowi = lax.broadcasted_iota(jnp.int32, (HGRN_DIAG, LANE), 0)
    lanei = lax.broadcasted_iota(jnp.int32, (HGRN_DIAG, LANE), 1)
    diag_cols = []
    for b in range(n // HGRN_DIAG):
        rel = lanei - b * HGRN_DIAG
        causal = (rowi <= rel) if rev else (rowi >= rel)
        diag_cols.append(jnp.where(causal, rel, -1))

    def chunk(j, carry):
        r0 = pl.multiple_of(((sub - 1 - j) if rev else j) * n, n)
        rows = pl.ds(r0, n)
        raw = f_ref[rows, :].astype(F32)
        t = jnp.exp(-jnp.abs(raw))
        r = 1.0 / (1.0 + t)
        nonneg = raw >= 0.0
        sig = jnp.where(nonneg, r, t * r)
        sig_neg = jnp.where(nonneg, t * r, r)
        om = om_ref[...]
        logf = jnp.log(lbf_ref[...] + om * sig)
        kk = om * sig_neg
        u = _split_dot(tri_ref[...], logf * LOG2E)
        qf = q_ref[rows, :].astype(F32) * (H_KEY ** -0.5)
        v = i_ref[rows, :].astype(F32)
        k_s[...] = kk
        u_s[...] = u

        u_end = u[0:1] if rev else u[n - 1:n]
        qb = (qf * jnp.exp2(u)).astype(BF16)
        kw = (kk * jnp.exp2(u_end - u)).astype(BF16)
        cd = jnp.exp2(u_end)

        levels = []
        for m, msk in level_masks:
            refs_rows = []
            for p in range(n // (2 * m)):
                ridx = 2 * m * p + (m if rev else m - 1)
                refs_rows.append(jnp.broadcast_to(u[ridx:ridx + 1], (2 * m, H_WIDTH)))
            rb = refs_rows[0] if len(refs_rows) == 1 else jnp.concatenate(refs_rows, axis=0)
            qt = (qf * jnp.exp2(u - rb)).astype(BF16)
            kt = (kk * jnp.exp2(rb - u)).astype(BF16)
            levels.append((qt, kt, msk))

        def diag_scores(h):
            sl = slice(h * LANE, (h + 1) * LANE)
            pieces = []
            for b in range(n // HGRN_DIAG):
                b0 = b * HGRN_DIAG
                q_blk = qf[b0:b0 + HGRN_DIAG, sl]
                u_blk = u[b0:b0 + HGRN_DIAG, sl]
                a_blk = jnp.zeros((HGRN_DIAG, LANE), F32)
                for s in range(HGRN_DIAG):
                    k_row = k_s[b0 + s:b0 + s + 1, sl]
                    u_row = u_s[b0 + s:b0 + s + 1, sl]
                    w = q_blk * (k_row * jnp.exp2(u_blk - u_row))
                    att = jnp.sum(w, axis=-1, keepdims=True)
                    a_blk = jnp.where(diag_cols[b] == s, att, a_blk)
                pieces.append(a_blk)
            return jnp.concatenate(pieces, axis=0)

        for h in range(H_HEADS):
            sl = slice(h * LANE, (h + 1) * LANE)
            att = diag_scores(h)
            for qt, kt, msk in levels:
                att = att + jnp.where(msk, _dot_nt(qt[:, sl], kt[:, sl]), 0.0)
            st = st_ref[h]
            v_t = v[:, sl].T.astype(BF16)
            lhs = jnp.concatenate([att.astype(BF16), qb[:, sl]], axis=1)
            rhs = jnp.concatenate([v_t, st.astype(BF16)], axis=1)
            oh = _dot_nt(lhs, rhs)
            st_ref[h] = st * cd[:, sl] + _dot(v_t, kw[:, sl])
            if not rev:
                o_ref[rows, sl] = oh
            else:
                oh = oh + yin_ref[rows, sl]
                ms = jnp.mean(oh * oh, axis=-1, keepdims=True)
                gf = g_ref[rows, sl].astype(F32)
                o_ref[rows, sl] = (oh * lax.rsqrt(ms + EPS) * nw_ref[...]
                                   * (gf * _sigmoid(gf))).astype(o_ref.dtype)
        return carry

    lax.fori_loop(0, sub, chunk, 0)


def _hgrn(rev, nb, seq, hin, lbf, om, tri, extra):
    sub = _chunks_per_step(seq // HGRN_CHUNK, CHUNKS_PER_STEP)
    n = sub * HGRN_CHUNK
    nc = seq // n
    t = nb * seq

    def blk(col):
        return lambda b, c: (b * nc + (nc - 1 - c if rev else c), col)

    const = lambda b, c: (0, 0)
    in_specs = [
        pl.BlockSpec((n, H_WIDTH), blk(0)),
        pl.BlockSpec((n, H_WIDTH), blk(2 if rev else 1)),
        pl.BlockSpec((n, H_WIDTH), blk(3)),
        pl.BlockSpec((1, H_WIDTH), const),
        pl.BlockSpec((1, H_WIDTH), const),
        pl.BlockSpec((HGRN_CHUNK, HGRN_CHUNK), const),
    ]
    args = [hin, hin, hin, lbf, om, tri]
    if rev:
        y_in, nw = extra
        in_specs += [pl.BlockSpec((n, H_WIDTH), blk(0)), pl.BlockSpec((n, H_WIDTH), blk(4)),
                     pl.BlockSpec((1, LANE), const)]
        args += [y_in, hin, nw]
        out_dtype = BF16
    else:
        out_dtype = F32
    return pl.pallas_call(
        functools.partial(_hgrn_kernel, rev, sub),
        grid=(nb, nc),
        in_specs=in_specs,
        out_specs=pl.BlockSpec((n, H_WIDTH), blk(0)),
        out_shape=jax.ShapeDtypeStruct((t, H_WIDTH), out_dtype),
        scratch_shapes=[pltpu.VMEM((H_HEADS, LANE, LANE), F32)]
        + [pltpu.VMEM((HGRN_CHUNK, H_WIDTH), F32) for _ in range(2)],
        compiler_params=_params(dimension_semantics=("arbitrary", "arbitrary")),
        name="hgrn_rev" if rev else "hgrn_fwd",
    )(*args)


def _attn_kernel(tq, tk, bounded_ref, q_ref, k_ref, vt_ref, o_ref, m_s, acc_s):
    nk = k_ref.shape[0] // tk
    for g in range(A_KV):
        gl = slice(g * LANE, (g + 1) * LANE)
        qs = jnp.concatenate(
            [q_ref[:, (g * A_GROUP + r) * LANE:(g * A_GROUP + r + 1) * LANE] for r in range(A_GROUP)], axis=0)
        acc_s[...] = jnp.zeros_like(acc_s)

        def tiles(kt):
            k0 = pl.multiple_of(kt * tk, tk)
            return k_ref[pl.ds(k0, tk), gl], vt_ref[gl, pl.ds(k0, tk)]

        def shifted_body(kt, carry):
            kb, vtb = tiles(kt)
            acc_s[...] += _dot(vtb, jnp.exp2(_dot_nt(kb, qs)).astype(BF16))
            return carry

        def online_body(kt, carry):
            kb, vtb = tiles(kt)
            s = _dot_nt(kb, qs)
            m_prev = m_s[...]
            m_new = jnp.maximum(m_prev, jnp.max(s, axis=0, keepdims=True))
            p = jnp.exp2(s - m_new)
            acc_s[...] = jnp.exp2(m_prev - m_new) * acc_s[...] + _dot(vtb, p.astype(BF16))
            m_s[...] = m_new
            return carry

        @pl.when(bounded_ref[0] != 0)
        def _():
            lax.fori_loop(0, nk, shifted_body, 0, unroll=min(4, nk))

        @pl.when(bounded_ref[0] == 0)
        def _():
            m_s[...] = jnp.full_like(m_s, -jnp.inf)
            lax.fori_loop(0, nk, online_body, 0)

        o = acc_s[...]
        o = o[:A_HD] / o[A_HD:A_HD + 1]
        for pr in range(A_GROUP // 2):
            pair = jnp.concatenate([o[:, (2 * pr) * tq:(2 * pr + 1) * tq],
                                    o[:, (2 * pr + 1) * tq:(2 * pr + 2) * tq]], axis=0)
            c0 = g * A_GROUP * A_HD + pr * LANE
            o_ref[:, c0:c0 + LANE] = pair.T.astype(o_ref.dtype)


def _attn(nb, seq, bounded, qh, kh, vt, tq, tk):
    t = nb * seq
    nq = seq // tq
    cols = A_GROUP * tq
    return pl.pallas_call(
        functools.partial(_attn_kernel, tq, tk),
        grid=(nb, nq),
        in_specs=[
            pl.BlockSpec(memory_space=pltpu.SMEM),
            pl.BlockSpec((tq, A_HEADS * LANE), lambda b, i: (b * nq + i, 0)),
            pl.BlockSpec((seq, A_KV * LANE), lambda b, i: (b, 0)),
            pl.BlockSpec((A_KV * LANE, seq), lambda b, i: (0, b)),
        ],
        out_specs=pl.BlockSpec((tq, A_HEADS * A_HD), lambda b, i: (b * nq + i, 0)),
        out_shape=jax.ShapeDtypeStruct((t, A_HEADS * A_HD), BF16),
        scratch_shapes=[pltpu.VMEM((1, cols), F32), pltpu.VMEM((LANE, cols), F32)],
        compiler_params=_params(dimension_semantics=("parallel", "parallel")),
        name="attn",
    )(bounded, qh, kh, vt)


def _merge_kernel(x_ref, ym_ref, yh_ref, ya_ref, g_ref, wb_ref, wo_ref, nw_ref, o_ref):
    mixed = None
    for i, y_ref in enumerate((ym_ref, yh_ref, ya_ref)):
        proj = _dot(y_ref[...], wb_ref[i])
        gate = _sigmoid(g_ref[:, i * D_MODEL:(i + 1) * D_MODEL].astype(F32))
        mixed = gate * proj if mixed is None else mixed + gate * proj
    out = _dot(mixed.astype(BF16), wo_ref[...])
    o_ref[...] = x_ref[...] + _rms(out, nw_ref[...])


def _merge(x, ym, yh, ya, gates, wb, wo, nw, tm):
    t = x.shape[0]
    row = lambda i: (i, 0)
    return pl.pallas_call(
        _merge_kernel,
        grid=(t // tm,),
        in_specs=[
            pl.BlockSpec((tm, D_MODEL), row),
            pl.BlockSpec((tm, M_INNER), row),
            pl.BlockSpec((tm, H_WIDTH), row),
            pl.BlockSpec((tm, A_HEADS * A_HD), row),
            pl.BlockSpec((tm, 3 * D_MODEL), row),
            pl.BlockSpec((3, M_INNER, D_MODEL), lambda i: (0, 0, 0)),
            pl.BlockSpec((D_MODEL, D_MODEL), lambda i: (0, 0)),
            pl.BlockSpec((1, D_MODEL), lambda i: (0, 0)),
        ],
        out_specs=pl.BlockSpec((tm, D_MODEL), row),
        out_shape=jax.ShapeDtypeStruct((t, D_MODEL), F32),
        compiler_params=_params(dimension_semantics=("parallel",)),
        name="merge",
    )(x, ym, yh, ya, gates, wb, wo, nw)


def _ffn_kernel(tiles_per_seq, xp_ref, xm_ref, xn_ref, nw1_ref, wup_ref, cw_ref, cb_ref, wdn_ref,
                nw2_ref, o_ref, u_s, act_s):
    j = pl.program_id(0) % tiles_per_seq
    tm = xm_ref.shape[0]
    x = xm_ref[...]
    nw1 = nw1_ref[...]
    hp = jnp.where(j > 0, _rms(xp_ref[...], nw1), 0.0)
    hn = jnp.where(j < tiles_per_seq - 1, _rms(xn_ref[...], nw1), 0.0)
    hext = jnp.concatenate([hp, _rms(x, nw1), hn], axis=0).astype(BF16)

    def up(slot, c0):
        u_s[slot] = _dot(hext, wup_ref[:, c0:c0 + FFN_COLS])

    def conv(slot, c0):
        w = cw_ref[:, c0:c0 + FFN_COLS]
        out = cb_ref[:, c0:c0 + FFN_COLS]
        for k in range(3):
            out = out + u_s[slot, pl.ds(FFN_HALO - 1 + k, tm), :] * w[k:k + 1]
        return out

    n_chunks = D_FF // FFN_COLS
    acc = None
    group_start = 0
    up(0, 0)
    up(1, D_FF)
    for cc in range(n_chunks):
        c0 = cc * FFN_COLS
        par = 2 * (cc % 2)
        if cc + 1 < n_chunks:
            up(2 - par, c0 + FFN_COLS)
            up(3 - par, D_FF + c0 + FFN_COLS)
        ug = conv(par, c0)
        uu = conv(par + 1, D_FF + c0)
        act_s[:, c0:c0 + FFN_COLS] = (ug * _sigmoid(ug) * uu).astype(BF16)
        if (cc + 1) % FFN_DOWN_GROUP == 0 or cc + 1 == n_chunks:
            k0, k1 = group_start * FFN_COLS, c0 + FFN_COLS
            part = _dot(act_s[:, k0:k1], wdn_ref[k0:k1, :])
            acc = part if acc is None else acc + part
            group_start = cc + 1
    o_ref[...] = x + _rms(acc, nw2_ref[...])


def _ffn(x, nw1, wup, cw, cb, wdn, nw2, seq, tm):
    t = x.shape[0]
    tiles_per_seq = seq // tm
    hb = tm // FFN_HALO
    last_halo = t // FFN_HALO - 1
    const = lambda i: (0, 0)
    single = pl.Buffered(1)
    return pl.pallas_call(
        functools.partial(_ffn_kernel, tiles_per_seq),
        grid=(t // tm,),
        in_specs=[
            pl.BlockSpec((FFN_HALO, D_MODEL), lambda i: (jnp.maximum(i * hb - 1, 0), 0)),
            pl.BlockSpec((tm, D_MODEL), lambda i: (i, 0)),
            pl.BlockSpec((FFN_HALO, D_MODEL), lambda i: (jnp.minimum(i * hb + hb, last_halo), 0)),
            pl.BlockSpec((1, D_MODEL), const),
            pl.BlockSpec((D_MODEL, 2 * D_FF), const, pipeline_mode=single),
            pl.BlockSpec((3, 2 * D_FF), const),
            pl.BlockSpec((1, 2 * D_FF), const),
            pl.BlockSpec((D_FF, D_MODEL), const, pipeline_mode=single),
            pl.BlockSpec((1, D_MODEL), const),
        ],
        out_specs=pl.BlockSpec((tm, D_MODEL), lambda i: (i, 0)),
        out_shape=jax.ShapeDtypeStruct((t, D_MODEL), F32),
        scratch_shapes=[pltpu.VMEM((4, tm + 2 * FFN_HALO, FFN_COLS), F32), pltpu.VMEM((tm, D_FF), BF16)],
        compiler_params=_params(dimension_semantics=("parallel",)),
        name="ffn",
    )(x, x, x, nw1, wup, cw, cb, wdn, nw2)


def _pack_w_in(w):
    sizes = (M_INNER, M_XBC, M_HEADS, M_HEADS, H_WIDTH, H_WIDTH, H_WIDTH, H_WIDTH, H_WIDTH,
             A_HEADS * A_HD, A_KV * A_HD, A_KV * A_HD, 3 * D_MODEL)
    parts, off = [], 0
    for s in sizes:
        parts.append(w[:, off:off + s])
        off += s
    (m_z, m_xbc, m_dtf, m_dtb, h_q, h_ff, h_fb, h_i, h_g, a_q, a_k, a_v, gate) = parts
    dt = jnp.pad(jnp.concatenate([m_dtf, m_dtb], axis=1), ((0, 0), (0, LANE - 2 * M_HEADS)))
    packed = jnp.concatenate(
        [m_z, m_xbc, dt, h_q, h_ff, h_fb, h_i, h_g, a_q, a_k, a_v, gate], axis=1)
    return packed.astype(BF16)


def _pad_lane(v):
    v = v.reshape(1, -1)
    return jnp.pad(v, ((0, 0), (0, LANE - v.shape[1])))


def _rope_tables(n):
    pos = jnp.arange(n)
    row = (pos // GRID_W).astype(F32)
    col = (pos % GRID_W).astype(F32)
    half = A_HD // 4
    inv = ROPE_BASE ** (-jnp.arange(0, 2 * half, 2, dtype=F32) / (2 * half))
    ar = row[:, None] * inv
    ac = col[:, None] * inv
    cr, sr, cc, sc = jnp.cos(ar), jnp.sin(ar), jnp.cos(ac), jnp.sin(ac)
    cos = jnp.concatenate([cr, cr, cc, cc] * (LANE // A_HD), axis=1)
    sin = jnp.concatenate([-sr, sr, -sc, sc] * (LANE // A_HD), axis=1)
    return cos, sin


def _tri(n, rev):
    i = jnp.arange(n)
    m = (i[None, :] >= i[:, None]) if rev else (i[None, :] <= i[:, None])
    return m.astype(BF16)


def _pick_tile(seq, want):
    return want if seq % want == 0 else seq


def kernel(x_prompt, x_sample, norm_mix_pre, w_in, m_conv_w, m_conv_b, m_dt_bias, m_a_log, m_d, m_norm_w, h_lb_logits, h_norm_w, a_q_norm, a_k_norm, w_branch, w_out, norm_mix_post, norm_ffn_pre, f_w_up, f_conv_w, f_conv_b, f_w_down, norm_ffn_post):
    sm = jax.nn.softmax(h_lb_logits.astype(F32), axis=0)
    tail = jnp.concatenate([jnp.zeros_like(sm[:1]), sm[1:]], axis=0)
    lower = jnp.cumsum(tail, axis=0)
    one_minus = sm[0:1] + (jnp.sum(tail, axis=0, keepdims=True) - lower)
    lower_floor = jnp.maximum(lower, LB_FLOOR)

    lane_a = jnp.arange(LANE) == A_HD
    kadd = lane_a.astype(F32).reshape(1, LANE)
    layers = []
    for l in range(DEPTH):
        score_bound = A_HD ** 0.5 * jnp.max(jnp.abs(a_q_norm[l])) * jnp.max(jnp.abs(a_k_norm[l]))
        bounded = score_bound <= ATT_SHIFT_LIMIT
        shift = jnp.where(bounded, score_bound * LOG2E, 0.0)
        layers.append(dict(
            bounded=bounded.astype(jnp.int32).reshape(1),
            qadd=jnp.where(lane_a, -shift, 0.0).astype(F32).reshape(1, LANE), kadd=kadd,
            nw_pre=norm_mix_pre[l].reshape(1, D_MODEL),
            w_in=_pack_w_in(w_in[l]),
            cw=m_conv_w[l], cb=m_conv_b[l].reshape(1, M_XBC),
            dtb=_pad_lane(m_dt_bias[l]), a_neg=_pad_lane(-jnp.exp(m_a_log[l].astype(F32)) * LOG2E),
            dskip=jnp.repeat(m_d[l], M_INNER // M_HEADS).reshape(1, M_INNER),
            m_nw=m_norm_w[l].reshape(1, M_INNER),
            lbf=lower_floor[l], om=one_minus[l],
            h_nw=h_norm_w[l].reshape(1, LANE),
            qw=jnp.tile(a_q_norm[l] * (A_HD ** -0.5 * LOG2E), LANE // A_HD).reshape(1, LANE),
            kw=jnp.tile(a_k_norm[l], LANE // A_HD).reshape(1, LANE),
            wb=w_branch[l].astype(BF16), wo=w_out[l].astype(BF16),
            nw_post=norm_mix_post[l].reshape(1, D_MODEL),
            nw_ffn=norm_ffn_pre[l].reshape(1, D_MODEL),
            wup=f_w_up[l].astype(BF16), fcw=f_conv_w[l], fcb=f_conv_b[l].reshape(1, 2 * D_FF),
            wdn=f_w_down[l].astype(BF16),
            nw_ffn_post=norm_ffn_post[l].reshape(1, D_MODEL),
        ))
    tri_ssd = (_tri(SSD_CHUNK, False), _tri(SSD_CHUNK, True))
    tri_hgrn = (_tri(HGRN_CHUNK, False), _tri(HGRN_CHUNK, True))
    head_of_lane = jnp.arange(M_INNER) // (M_INNER // M_HEADS)
    expand_ssd = tuple(
        (jnp.arange(LANE)[:, None] == d * M_HEADS + head_of_lane[None, :]).astype(BF16) for d in range(2))

    def run(x):
        nb, seq, _ = x.shape
        xf = x.reshape(nb * seq, D_MODEL)
        cos, sin = _rope_tables(seq)
        tm = _pick_tile(seq, 512)
        tq = _pick_tile(seq, 512)
        tk = _pick_tile(seq, 512)
        for p in layers:
            z, xbc, dtraw, hin, qh, kh, vh, gates = _inproj(
                xf, p["nw_pre"], p["w_in"], cos, sin, p["qw"], p["kw"],
                p["qadd"], p["kadd"], seq, tm)
            y_f, xc = _ssd_fwd(nb, seq, xbc, dtraw, p["cw"], p["cb"], p["dtb"], p["a_neg"],
                               tri_ssd[0], expand_ssd[0], p["dskip"])
            y_m = _ssd_rev(nb, seq, xc, dtraw, p["dtb"], p["a_neg"], tri_ssd[1], expand_ssd[1],
                           y_f, z, p["m_nw"])
            o_f = _hgrn(False, nb, seq, hin, p["lbf"][0:1], p["om"][0:1], tri_hgrn[0], None)
            y_h = _hgrn(True, nb, seq, hin, p["lbf"][1:2], p["om"][1:2], tri_hgrn[1], (o_f, p["h_nw"]))
            y_a = _attn(nb, seq, p["bounded"], qh, kh, vh, tq, tk)
            xf = _merge(xf, y_m, y_h, y_a, gates, p["wb"], p["wo"], p["nw_post"], tm)
            xf = _ffn(xf, p["nw_ffn"], p["wup"], p["fcw"], p["fcb"], p["wdn"], p["nw_ffn_post"], seq, tm)
        return xf.reshape(nb, seq, D_MODEL)

    return (run(x_prompt), run(x_sample))
```

```python
import functools

import jax
import jax.numpy as jnp
from jax import lax
from jax.experimental import pallas as pl
from jax.experimental.pallas import tpu as pltpu

F32 = jnp.float32
BF16 = jnp.bfloat16

D_MODEL = 1024
DEPTH = 4
EPS = 1e-6
LB_FLOOR = 1e-30
GRID_W = 64
ROPE_BASE = 10000.0

M_HEADS = 8
M_INNER = 512
M_STATE = 64
M_CONV = 4
M_XBC = 768
SSD_CHUNK = 128
CHUNKS_PER_STEP = 4

H_HEADS = 4
H_KEY = 128
H_WIDTH = 512
HGRN_CHUNK = 128
HGRN_DIAG = 8

A_HEADS = 8
A_KV = 2
A_HD = 64
A_GROUP = A_HEADS // A_KV
A_VT_ROWS = 80

D_FF = 2816
FFN_COLS = 256
FFN_DOWN_GROUP = 4
FFN_HALO = 8
XBC_HALO = 16

LANE = 128
LOG2E = 1.4426950408889634
ATT_SCORE_LIMIT = 40.0
VMEM_LIMIT = 56 * 1024 * 1024

OFF_Z = 0
OFF_XBC = OFF_Z + M_INNER
OFF_DT = OFF_XBC + M_XBC
OFF_H = OFF_DT + LANE
OFF_Q = OFF_H + 5 * H_WIDTH
OFF_K = OFF_Q + A_HEADS * A_HD
OFF_V = OFF_K + A_KV * A_HD
OFF_G = OFF_V + A_KV * A_HD
N_PACKED = OFF_G + 3 * D_MODEL


def _rms(x, w):
    return x * lax.rsqrt(jnp.mean(x * x, axis=-1, keepdims=True) + EPS) * w


def _sigmoid(x):
    return 1.0 / (1.0 + jnp.exp(-x))


def _dot(a, b):
    return jnp.dot(a, b, preferred_element_type=F32)


def _dot_nt(a, b):
    return lax.dot_general(a, b, (((1,), (1,)), ((), ())), preferred_element_type=F32)


def _split_dot(tri, x):
    hi = x.astype(BF16)
    lo = (x - hi.astype(F32)).astype(BF16)
    return _dot(tri, hi) + _dot(tri, lo)


def _params(**kw):
    return pltpu.CompilerParams(vmem_limit_bytes=VMEM_LIMIT, **kw)


def _inproj_kernel(x_ref, nw_ref, w_ref, cos_ref, sin_ref, qw_ref, kw_ref,
                   oz, oxbc, odt, oh, oq, ok, ovt, og):
    h = _rms(x_ref[...], nw_ref[...]).astype(BF16)

    def mm(c0, n):
        return _dot(h, w_ref[:, c0:c0 + n])

    def plain(o_ref, c0, width, step):
        for j in range(0, width, step):
            o_ref[:, j:j + step] = mm(c0 + j, step).astype(o_ref.dtype)

    plain(oz, OFF_Z, M_INNER, 512)
    plain(oxbc, OFF_XBC, M_XBC, 256)
    odt[...] = mm(OFF_DT, LANE)
    plain(oh, OFF_H, 5 * H_WIDTH, 512)

    cos = cos_ref[...]
    sin = sin_ref[...]
    lane = lax.broadcasted_iota(jnp.int32, cos.shape, 1)
    first_half = (lane & 16) == 0
    low = lane < A_HD

    def split_pair(y):
        return jnp.where(low, y, 0.0), jnp.where(low, pltpu.roll(y, A_HD, 1), 0.0)

    def norm_rope(y, w):
        sq = y * y
        ss_lo = jnp.sum(jnp.where(low, sq, 0.0), axis=-1, keepdims=True)
        ss_hi = jnp.sum(jnp.where(low, 0.0, sq), axis=-1, keepdims=True)
        yn = y * lax.rsqrt(jnp.where(low, ss_lo, ss_hi) * (1.0 / A_HD) + EPS) * w
        partner = jnp.where(first_half, pltpu.roll(yn, LANE - 16, 1), pltpu.roll(yn, 16, 1))
        return yn * cos + partner * sin

    qw = qw_ref[...]
    yq = mm(OFF_Q, A_HEADS * A_HD)
    for pr in range(A_HEADS // 2):
        a, b = split_pair(norm_rope(yq[:, pr * LANE:(pr + 1) * LANE], qw))
        oq[:, 2 * pr * LANE:(2 * pr + 1) * LANE] = a.astype(BF16)
        oq[:, (2 * pr + 1) * LANE:(2 * pr + 2) * LANE] = b.astype(BF16)
    ykv = mm(OFF_K, 2 * A_KV * A_HD)
    a, b = split_pair(norm_rope(ykv[:, :LANE], kw_ref[...]))
    ok[:, :LANE] = a.astype(BF16)
    ok[:, LANE:] = b.astype(BF16)
    ones_lane = jnp.where(lane == A_HD, 1.0, 0.0)
    a, b = split_pair(ykv[:, LANE:])
    ovt[:LANE, :] = (a + ones_lane).T.astype(BF16)
    ovt[LANE:, :] = (b + ones_lane).T.astype(BF16)
    plain(og, OFF_G, 3 * D_MODEL, 512)


def _inproj(x, nw, w, cos, sin, qw, kw, seq, tm):
    t = x.shape[0]
    tiles_per_seq = seq // tm
    row = lambda i: (i, 0)
    const = lambda i: (0, 0)
    pos = lambda i: (i % tiles_per_seq, 0)
    outs = ((M_INNER, BF16), (M_XBC, BF16), (LANE, F32), (5 * H_WIDTH, BF16), (A_HEADS * LANE, BF16),
            (A_KV * LANE, BF16), None, (3 * D_MODEL, BF16))
    out_specs = [pl.BlockSpec((A_KV * LANE, tm), lambda i: (0, i)) if o is None
                 else pl.BlockSpec((tm, o[0]), row) for o in outs]
    out_shape = [jax.ShapeDtypeStruct((A_KV * LANE, t), BF16) if o is None
                 else jax.ShapeDtypeStruct((t, o[0]), o[1]) for o in outs]
    return pl.pallas_call(
        _inproj_kernel,
        grid=(t // tm,),
        in_specs=[
            pl.BlockSpec((tm, D_MODEL), row),
            pl.BlockSpec((1, D_MODEL), const),
            pl.BlockSpec((D_MODEL, N_PACKED), const, pipeline_mode=pl.Buffered(1)),
            pl.BlockSpec((tm, LANE), pos),
            pl.BlockSpec((tm, LANE), pos),
            pl.BlockSpec((1, LANE), const),
            pl.BlockSpec((1, LANE), const),
        ],
        out_specs=out_specs,
        out_shape=out_shape,
        compiler_params=_params(dimension_semantics=("parallel",)),
        name="inproj",
    )(x, nw, w, cos, sin, qw, kw)


def _ssd_kernel(rev, nblk, sub, *refs):
    if rev:
        (xc_ref, dt_ref, dtb_ref, a_ref, tri_ref, e_ref, yin_ref, z_ref, nw_ref, o_ref, s_ref) = refs
    else:
        (xp_ref, xm_ref, xn_ref, dt_ref, cw_ref, cb_ref, dtb_ref, a_ref, tri_ref, e_ref,
         dsk_ref, o_ref, oxc_ref, s_ref, xc_s) = refs
    c = pl.program_id(1)
    q = SSD_CHUNK

    @pl.when(c == 0)
    def _():
        s_ref[...] = jnp.zeros_like(s_ref)

    if not rev:
        xp = jnp.where(c > 0, xp_ref[...].astype(F32), 0.0)
        xn = jnp.where(c < nblk - 1, xn_ref[...].astype(F32), 0.0)
        xe = jnp.concatenate([xp, xm_ref[...].astype(F32), xn], axis=0)
        cw = cw_ref[...]
        y = cb_ref[...]
        for k in range(M_CONV):
            start = XBC_HALO - M_CONV // 2 + k
            y = y + xe[start:start + sub * q] * cw[k:k + 1]
        xc_all = y * _sigmoid(y)
        xc_s[...] = xc_all
        oxc_ref[...] = xc_all.astype(oxc_ref.dtype)

    li = lax.broadcasted_iota(jnp.int32, (q, q), 0)
    si = lax.broadcasted_iota(jnp.int32, (q, q), 1)
    mask = (si >= li) if rev else (si <= li)
    lane_s = lax.broadcasted_iota(jnp.int32, (q, LANE), 1)
    lane_x = lax.broadcasted_iota(jnp.int32, (q, 2 * LANE), 1)
    base = M_HEADS if rev else 0
    hpg = M_HEADS // 2

    def chunk(j, carry):
        r0 = pl.multiple_of(((sub - 1 - j) if rev else j) * q, q)
        rows = pl.ds(r0, q)
        xc = xc_ref[rows, :].astype(F32) if rev else xc_s[rows, :]
        xs = xc[:, :M_INNER]
        bm = xc[:, M_INNER:M_INNER + LANE]
        cm = xc[:, M_INNER + LANE:]

        dtr = dt_ref[rows, :] + dtb_ref[...]
        dt = jnp.maximum(dtr, 0.0) + jnp.log1p(jnp.exp(-jnp.abs(dtr)))
        a = dt * a_ref[...]
        u = _split_dot(tri_ref[...], a)
        wide = _expand_heads(jnp.concatenate([u, dt], axis=0), e_ref[...])
        u_w = wide[:q]
        dt_w = wide[q:]
        u_end_w = u_w[0:1] if rev else u_w[q - 1:q]
        off_w = jnp.exp2(u_w)
        w_state_w = dt_w * jnp.exp2(u_end_w - u_w)
        cd_w = jnp.exp2(u_end_w)
        u_t = u.T
        dt_t = dt.T

        ys = []
        for g in range(2):
            sl = slice(g * 2 * LANE, (g + 1) * 2 * LANE)
            gsel = (lane_s >= M_STATE) if g else (lane_s < M_STATE)
            cg = jnp.where(gsel, cm, 0.0).astype(BF16)
            bg = jnp.where(gsel, bm, 0.0).astype(BF16)
            cbm = _dot_nt(cg, bm.astype(BF16))
            xg = xs[:, sl]
            lhs = []
            for r in range(hpg):
                ln = base + g * hpg + r
                dec = jnp.exp2(jnp.where(mask, u[:, ln:ln + 1] - u_t[ln:ln + 1, :], -1e30))
                lhs.append((cbm * dec * dt_t[ln:ln + 1, :]).astype(BF16))
            xr = [jnp.where((lane_x >> 6) == r, xg, 0.0).astype(BF16) for r in range(hpg)]
            y_diag = _dot(jnp.concatenate(lhs, axis=1), jnp.concatenate(xr, axis=0))
            sg = s_ref[g]
            ys.append(y_diag + _dot(cg, sg.astype(BF16)) * off_w[:, sl])
            st = lax.dot_general(bg, (xg * w_state_w[:, sl]).astype(BF16),
                                 (((0,), (0,)), ((), ())), preferred_element_type=F32)
            s_ref[g] = sg * cd_w[:, sl] + st
        yc = jnp.concatenate(ys, axis=1)
        if not rev:
            o_ref[rows, :] = yc + xs * dsk_ref[...]
        else:
            yt = yin_ref[rows, :] + yc
            zf = z_ref[rows, :].astype(F32)
            yt = yt * (zf * _sigmoid(zf))
            nw = nw_ref[...]
            for g in range(2):
                sl = slice(g * 2 * LANE, (g + 1) * 2 * LANE)
                blk = yt[:, sl]
                ms = jnp.mean(blk * blk, axis=-1, keepdims=True)
                o_ref[rows, sl] = (blk * lax.rsqrt(ms + EPS) * nw[:, sl]).astype(o_ref.dtype)
        return carry

    lax.fori_loop(0, sub, chunk, 0, unroll=True)


def _expand_heads(x, e):
    hi = x.astype(BF16)
    rest = x - hi.astype(F32)
    mid = rest.astype(BF16)
    lo = (rest - mid.astype(F32)).astype(BF16)
    return _dot(hi, e) + _dot(mid, e) + _dot(lo, e)


def _chunks_per_step(n_chunks, want):
    while n_chunks % want:
        want -= 1
    return want


def _ssd_fwd(nb, seq, xbc, dtraw, cw, cb, dtb, a_neg, tri, expand, dskip):
    q = SSD_CHUNK
    sub = _chunks_per_step(seq // q, CHUNKS_PER_STEP)
    rows = sub * q
    nblk = seq // rows
    t = nb * seq
    hb = rows // XBC_HALO
    last_halo = t // XBC_HALO - 1
    main = lambda b, c: (b * nblk + c, 0)
    prev = lambda b, c: (jnp.maximum((b * nblk + c) * hb - 1, 0), 0)
    nxt = lambda b, c: (jnp.minimum((b * nblk + c) * hb + hb, last_halo), 0)
    const = lambda b, c: (0, 0)
    return pl.pallas_call(
        functools.partial(_ssd_kernel, False, nblk, sub),
        grid=(nb, nblk),
        in_specs=[
            pl.BlockSpec((XBC_HALO, M_XBC), prev),
            pl.BlockSpec((rows, M_XBC), main),
            pl.BlockSpec((XBC_HALO, M_XBC), nxt),
            pl.BlockSpec((rows, LANE), main),
            pl.BlockSpec((M_CONV, M_XBC), const),
            pl.BlockSpec((1, M_XBC), const),
            pl.BlockSpec((1, LANE), const),
            pl.BlockSpec((1, LANE), const),
            pl.BlockSpec((q, q), const),
            pl.BlockSpec((LANE, M_INNER), const),
            pl.BlockSpec((1, M_INNER), const),
        ],
        out_specs=[pl.BlockSpec((rows, M_INNER), main), pl.BlockSpec((rows, M_XBC), main)],
        out_shape=[jax.ShapeDtypeStruct((t, M_INNER), F32), jax.ShapeDtypeStruct((t, M_XBC), BF16)],
        scratch_shapes=[pltpu.VMEM((2, LANE, 2 * LANE), F32), pltpu.VMEM((rows, M_XBC), F32)],
        compiler_params=_params(dimension_semantics=("arbitrary", "arbitrary")),
        name="ssd_fwd",
    )(xbc, xbc, xbc, dtraw, cw, cb, dtb, a_neg, tri, expand, dskip)


def _ssd_rev(nb, seq, xc, dtraw, dtb, a_neg, tri, expand, y_in, z, nw):
    q = SSD_CHUNK
    sub = _chunks_per_step(seq // q, CHUNKS_PER_STEP)
    rows = sub * q
    nblk = seq // rows
    t = nb * seq
    main = lambda b, c: (b * nblk + nblk - 1 - c, 0)
    const = lambda b, c: (0, 0)
    return pl.pallas_call(
        functools.partial(_ssd_kernel, True, nblk, sub),
        grid=(nb, nblk),
        in_specs=[
            pl.BlockSpec((rows, M_XBC), main),
            pl.BlockSpec((rows, LANE), main),
            pl.BlockSpec((1, LANE), const),
            pl.BlockSpec((1, LANE), const),
            pl.BlockSpec((q, q), const),
            pl.BlockSpec((LANE, M_INNER), const),
            pl.BlockSpec((rows, M_INNER), main),
            pl.BlockSpec((rows, M_INNER), main),
            pl.BlockSpec((1, M_INNER), const),
        ],
        out_specs=pl.BlockSpec((rows, M_INNER), main),
        out_shape=jax.ShapeDtypeStruct((t, M_INNER), BF16),
        scratch_shapes=[pltpu.VMEM((2, LANE, 2 * LANE), F32)],
        compiler_params=_params(dimension_semantics=("arbitrary", "arbitrary")),
        name="ssd_rev",
    )(xc, dtraw, dtb, a_neg, tri, expand, y_in, z, nw)


def _hgrn_kernel(rev, sub, *refs):
    if rev:
        (q_ref, f_ref, i_ref, lbf_ref, om_ref, tri_ref, yin_ref, g_ref, nw_ref,
         o_ref, st_ref, k_s, u_s) = refs
    else:
        (q_ref, f_ref, i_ref, lbf_ref, om_ref, tri_ref, o_ref, st_ref, k_s, u_s) = refs
    c = pl.program_id(1)
    n = HGRN_CHUNK

    @pl.when(c == 0)
    def _():
        st_ref[...] = jnp.zeros_like(st_ref)

    li = lax.broadcasted_iota(jnp.int32, (n, n), 0)
    si = lax.broadcasted_iota(jnp.int32, (n, n), 1)
    level_masks = []
    m = HGRN_DIAG
    while 2 * m <= n:
        sh = m.bit_length() - 1
        same_pair = (li >> (sh + 1)) == (si >> (sh + 1))
        q_half = ((li >> sh) & 1) == (0 if rev else 1)
        k_half = ((si >> sh) & 1) == (1 if rev else 0)
        level_masks.append((m, same_pair & q_half & k_half))
        m *= 2
    rowi = lax.broadcasted_iota(jnp.int32, (HGRN_DIAG, LANE), 0)
    lanei = lax.broadcasted_iota(jnp.int32, (HGRN_DIAG, LANE), 1)
    diag_cols = []
    for b in range(n // HGRN_DIAG):
        rel = lanei - b * HGRN_DIAG
        causal = (rowi <= rel) if rev else (rowi >= rel)
        diag_cols.append(jnp.where(causal, rel, -1))

    def chunk(j, carry):
        r0 = pl.multiple_of(((sub - 1 - j) if rev else j) * n, n)
        rows = pl.ds(r0, n)
        raw = f_ref[rows, :].astype(F32)
        t = jnp.exp(-jnp.abs(raw))
        r = 1.0 / (1.0 + t)
        nonneg = raw >= 0.0
        sig = jnp.where(nonneg, r, t * r)
        sig_neg = jnp.where(nonneg, t * r, r)
        om = om_ref[...]
        logf = jnp.log(lbf_ref[...] + om * sig)
        kk = om * sig_neg
        u = _split_dot(tri_ref[...], logf * LOG2E)
        qf = q_ref[rows, :].astype(F32) * (H_KEY ** -0.5)
        v = i_ref[rows, :].astype(F32)
        k_s[...] = kk
        u_s[...] = u

        u_end = u[0:1] if rev else u[n - 1:n]
        qb = (qf * jnp.exp2(u)).astype(BF16)
        kw = (kk * jnp.exp2(u_end - u)).astype(BF16)
        cd = jnp.exp2(u_end)

        levels = []
        for m, msk in level_masks:
            refs_rows = []
            for p in range(n // (2 * m)):
                ridx = 2 * m * p + (m if rev else m - 1)
                refs_rows.append(jnp.broadcast_to(u[ridx:ridx + 1], (2 * m, H_WIDTH)))
            rb = refs_rows[0] if len(refs_rows) == 1 else jnp.concatenate(refs_rows, axis=0)
            qt = (qf * jnp.exp2(u - rb)).astype(BF16)
            kt = (kk * jnp.exp2(rb - u)).astype(BF16)
            levels.append((qt, kt, msk))

        def diag_scores(h):
            sl = slice(h * LANE, (h + 1) * LANE)
            pieces = []
            for b in range(n // HGRN_DIAG):
                b0 = b * HGRN_DIAG
                q_blk = qf[b0:b0 + HGRN_DIAG, sl]
                u_blk = u[b0:b0 + HGRN_DIAG, sl]
                a_blk = jnp.zeros((HGRN_DIAG, LANE), F32)
                for s in range(HGRN_DIAG):
                    k_row = k_s[b0 + s:b0 + s + 1, sl]
                    u_row = u_s[b0 + s:b0 + s + 1, sl]
                    w = q_blk * (k_row * jnp.exp2(u_blk - u_row))
                    att = jnp.sum(w, axis=-1, keepdims=True)
                    a_blk = jnp.where(diag_cols[b] == s, att, a_blk)
                pieces.append(a_blk)
            return jnp.concatenate(pieces, axis=0)

        for h in range(H_HEADS):
            sl = slice(h * LANE, (h + 1) * LANE)
            att = diag_scores(h)
            for qt, kt, msk in levels:
                att = jnp.where(msk, _dot_nt(qt[:, sl], kt[:, sl]), att)
            st = st_ref[h]
            v_t = v[:, sl].T.astype(BF16)
            lhs = jnp.concatenate([att.astype(BF16), qb[:, sl]], axis=1)
            rhs = jnp.concatenate([v_t, st.astype(BF16)], axis=1)
            oh = _dot_nt(lhs, rhs)
            st_ref[h] = st * cd[:, sl] + _dot(v_t, kw[:, sl])
            if not rev:
                o_ref[rows, sl] = oh
            else:
                oh = oh + yin_ref[rows, sl]
                ms = jnp.mean(oh * oh, axis=-1, keepdims=True)
                gf = g_ref[rows, sl].astype(F32)
                o_ref[rows, sl] = (oh * lax.rsqrt(ms + EPS) * nw_ref[...]
                                   * (gf * _sigmoid(gf))).astype(o_ref.dtype)
        return carry

    lax.fori_loop(0, sub, chunk, 0, unroll=2)


def _hgrn(rev, nb, seq, hin, lbf, om, tri, extra):
    sub = _chunks_per_step(seq // HGRN_CHUNK, CHUNKS_PER_STEP)
    n = sub * HGRN_CHUNK
    nc = seq // n
    t = nb * seq

    def blk(col):
        return lambda b, c: (b * nc + (nc - 1 - c if rev else c), col)

    const = lambda b, c: (0, 0)
    in_specs = [
        pl.BlockSpec((n, H_WIDTH), blk(0)),
        pl.BlockSpec((n, H_WIDTH), blk(2 if rev else 1)),
        pl.BlockSpec((n, H_WIDTH), blk(3)),
        pl.BlockSpec((1, H_WIDTH), const),
        pl.BlockSpec((1, H_WIDTH), const),
        pl.BlockSpec((HGRN_CHUNK, HGRN_CHUNK), const),
    ]
    args = [hin, hin, hin, lbf, om, tri]
    if rev:
        y_in, nw = extra
        in_specs += [pl.BlockSpec((n, H_WIDTH), blk(0)), pl.BlockSpec((n, H_WIDTH), blk(4)),
                     pl.BlockSpec((1, LANE), const)]
        args += [y_in, hin, nw]
        out_dtype = BF16
    else:
        out_dtype = F32
    return pl.pallas_call(
        functools.partial(_hgrn_kernel, rev, sub),
        grid=(nb, nc),
        in_specs=in_specs,
        out_specs=pl.BlockSpec((n, H_WIDTH), blk(0)),
        out_shape=jax.ShapeDtypeStruct((t, H_WIDTH), out_dtype),
        scratch_shapes=[pltpu.VMEM((H_HEADS, LANE, LANE), F32)]
        + [pltpu.VMEM((HGRN_CHUNK, H_WIDTH), F32) for _ in range(2)],
        compiler_params=_params(dimension_semantics=("arbitrary", "arbitrary")),
        name="hgrn_rev" if rev else "hgrn_fwd",
    )(*args)


def _attn_kernel(tq, tk, bounded_ref, q_ref, k_ref, vt_ref, o_ref, m_s, acc_s):
    nk = k_ref.shape[0] // tk
    for g in range(A_KV):
        gl = slice(g * LANE, (g + 1) * LANE)
        qs = jnp.concatenate(
            [q_ref[:, (g * A_GROUP + r) * LANE:(g * A_GROUP + r + 1) * LANE] for r in range(A_GROUP)], axis=0)
        acc_s[...] = jnp.zeros_like(acc_s)

        def tiles(kt):
            k0 = pl.multiple_of(kt * tk, tk)
            return k_ref[pl.ds(k0, tk), gl], vt_ref[g * LANE:g * LANE + A_VT_ROWS, pl.ds(k0, tk)]

        def shifted_body(kt, carry):
            kb, vtb = tiles(kt)
            acc_s[...] += _dot(vtb, jnp.exp2(_dot_nt(kb, qs)).astype(BF16))
            return carry

        def online_body(kt, carry):
            kb, vtb = tiles(kt)
            s = _dot_nt(kb, qs)
            m_prev = m_s[...]
            m_new = jnp.maximum(m_prev, jnp.max(s, axis=0, keepdims=True))
            p = jnp.exp2(s - m_new)
            acc_s[...] = jnp.exp2(m_prev - m_new) * acc_s[...] + _dot(vtb, p.astype(BF16))
            m_s[...] = m_new
            return carry

        @pl.when(bounded_ref[0] != 0)
        def _():
            lax.fori_loop(0, nk, shifted_body, 0, unroll=min(4, nk))

        @pl.when(bounded_ref[0] == 0)
        def _():
            m_s[...] = jnp.full_like(m_s, -jnp.inf)
            lax.fori_loop(0, nk, online_body, 0)

        o = acc_s[...]
        o = o[:A_HD] / o[A_HD:A_HD + 1]
        for pr in range(A_GROUP // 2):
            pair = jnp.concatenate([o[:, (2 * pr) * tq:(2 * pr + 1) * tq],
                                    o[:, (2 * pr + 1) * tq:(2 * pr + 2) * tq]], axis=0)
            c0 = g * A_GROUP * A_HD + pr * LANE
            o_ref[:, c0:c0 + LANE] = pair.T.astype(o_ref.dtype)


def _attn(nb, seq, bounded, qh, kh, vt, tq, tk):
    t = nb * seq
    nq = seq // tq
    cols = A_GROUP * tq
    return pl.pallas_call(
        functools.partial(_attn_kernel, tq, tk),
        grid=(nb, nq),
        in_specs=[
            pl.BlockSpec(memory_space=pltpu.SMEM),
            pl.BlockSpec((tq, A_HEADS * LANE), lambda b, i: (b * nq + i, 0)),
            pl.BlockSpec((seq, A_KV * LANE), lambda b, i: (b, 0)),
            pl.BlockSpec((A_KV * LANE, seq), lambda b, i: (0, b)),
        ],
        out_specs=pl.BlockSpec((tq, A_HEADS * A_HD), lambda b, i: (b * nq + i, 0)),
        out_shape=jax.ShapeDtypeStruct((t, A_HEADS * A_HD), BF16),
        scratch_shapes=[pltpu.VMEM((1, cols), F32), pltpu.VMEM((A_VT_ROWS, cols), F32)],
        compiler_params=_params(dimension_semantics=("parallel", "parallel")),
        name="attn",
    )(bounded, qh, kh, vt)


def _merge_kernel(x_ref, ym_ref, yh_ref, ya_ref, g_ref, wb_ref, wo_ref, nw_ref, o_ref):
    mixed = None
    for i, y_ref in enumerate((ym_ref, yh_ref, ya_ref)):
        proj = _dot(y_ref[...], wb_ref[i])
        gate = _sigmoid(g_ref[:, i * D_MODEL:(i + 1) * D_MODEL].astype(F32))
        mixed = gate * proj if mixed is None else mixed + gate * proj
    out = _dot(mixed.astype(BF16), wo_ref[...])
    o_ref[...] = x_ref[...] + _rms(out, nw_ref[...])


def _merge(x, ym, yh, ya, gates, wb, wo, nw, tm):
    t = x.shape[0]
    row = lambda i: (i, 0)
    return pl.pallas_call(
        _merge_kernel,
        grid=(t // tm,),
        in_specs=[
            pl.BlockSpec((tm, D_MODEL), row),
            pl.BlockSpec((tm, M_INNER), row),
            pl.BlockSpec((tm, H_WIDTH), row),
            pl.BlockSpec((tm, A_HEADS * A_HD), row),
            pl.BlockSpec((tm, 3 * D_MODEL), row),
            pl.BlockSpec((3, M_INNER, D_MODEL), lambda i: (0, 0, 0)),
            pl.BlockSpec((D_MODEL, D_MODEL), lambda i: (0, 0)),
            pl.BlockSpec((1, D_MODEL), lambda i: (0, 0)),
        ],
        out_specs=pl.BlockSpec((tm, D_MODEL), row),
        out_shape=jax.ShapeDtypeStruct((t, D_MODEL), F32),
        compiler_params=_params(dimension_semantics=("parallel",)),
        name="merge",
    )(x, ym, yh, ya, gates, wb, wo, nw)


def _ffn_kernel(tiles_per_seq, xp_ref, xm_ref, xn_ref, nw1_ref, wup_ref, cw_ref, cb_ref, wdn_ref,
                nw2_ref, o_ref, act_s, *u_s):
    j = pl.program_id(0) % tiles_per_seq
    tm = xm_ref.shape[0]
    x = xm_ref[...]
    nw1 = nw1_ref[...]
    hp = jnp.where(j > 0, _rms(xp_ref[...], nw1), 0.0)
    hn = jnp.where(j < tiles_per_seq - 1, _rms(xn_ref[...], nw1), 0.0)
    hext = jnp.concatenate([hp, _rms(x, nw1), hn], axis=0).astype(BF16)

    def up(slot, c0):
        u_s[slot][...] = _dot(hext, wup_ref[:, c0:c0 + FFN_COLS])

    def conv(slot, c0):
        w = cw_ref[:, c0:c0 + FFN_COLS]
        out = cb_ref[:, c0:c0 + FFN_COLS]
        for k in range(3):
            out = out + u_s[slot][pl.ds(FFN_HALO - 1 + k, tm), :] * w[k:k + 1]
        return out

    n_chunks = D_FF // FFN_COLS
    acc = None
    group_start = 0
    up(0, 0)
    up(1, D_FF)
    for cc in range(n_chunks):
        c0 = cc * FFN_COLS
        par = 2 * (cc % 2)
        if cc + 1 < n_chunks:
            up(2 - par, c0 + FFN_COLS)
            up(3 - par, D_FF + c0 + FFN_COLS)
        ug = conv(par, c0)
        uu = conv(par + 1, D_FF + c0)
        act_s[:, c0:c0 + FFN_COLS] = (ug * _sigmoid(ug) * uu).astype(BF16)
        if (cc + 1) % FFN_DOWN_GROUP == 0 or cc + 1 == n_chunks:
            k0, k1 = group_start * FFN_COLS, c0 + FFN_COLS
            part = _dot(act_s[:, k0:k1], wdn_ref[k0:k1, :])
            acc = part if acc is None else acc + part
            group_start = cc + 1
    o_ref[...] = x + _rms(acc, nw2_ref[...])


def _ffn(x, nw1, wup, cw, cb, wdn, nw2, seq, tm):
    t = x.shape[0]
    tiles_per_seq = seq // tm
    hb = tm // FFN_HALO
    last_halo = t // FFN_HALO - 1
    const = lambda i: (0, 0)
    single = pl.Buffered(1)
    return pl.pallas_call(
        functools.partial(_ffn_kernel, tiles_per_seq),
        grid=(t // tm,),
        in_specs=[
            pl.BlockSpec((FFN_HALO, D_MODEL), lambda i: (jnp.maximum(i * hb - 1, 0), 0)),
            pl.BlockSpec((tm, D_MODEL), lambda i: (i, 0)),
            pl.BlockSpec((FFN_HALO, D_MODEL), lambda i: (jnp.minimum(i * hb + hb, last_halo), 0)),
            pl.BlockSpec((1, D_MODEL), const),
            pl.BlockSpec((D_MODEL, 2 * D_FF), const, pipeline_mode=single),
            pl.BlockSpec((3, 2 * D_FF), const),
            pl.BlockSpec((1, 2 * D_FF), const),
            pl.BlockSpec((D_FF, D_MODEL), const, pipeline_mode=single),
            pl.BlockSpec((1, D_MODEL), const),
        ],
        out_specs=pl.BlockSpec((tm, D_MODEL), lambda i: (i, 0)),
        out_shape=jax.ShapeDtypeStruct((t, D_MODEL), F32),
        scratch_shapes=[pltpu.VMEM((tm, D_FF), BF16)]
        + [pltpu.VMEM((tm + 2 * FFN_HALO, FFN_COLS), F32) for _ in range(4)],
        compiler_params=_params(dimension_semantics=("parallel",)),
        name="ffn",
    )(x, x, x, nw1, wup, cw, cb, wdn, nw2)


def _pack_w_in(w):
    sizes = (M_INNER, M_XBC, M_HEADS, M_HEADS, H_WIDTH, H_WIDTH, H_WIDTH, H_WIDTH, H_WIDTH,
             A_HEADS * A_HD, A_KV * A_HD, A_KV * A_HD, 3 * D_MODEL)
    parts, off = [], 0
    for s in sizes:
        parts.append(w[:, off:off + s])
        off += s
    (m_z, m_xbc, m_dtf, m_dtb, h_q, h_ff, h_fb, h_i, h_g, a_q, a_k, a_v, gate) = parts
    dt = jnp.pad(jnp.concatenate([m_dtf, m_dtb], axis=1), ((0, 0), (0, LANE - 2 * M_HEADS)))
    packed = jnp.concatenate(
        [m_z, m_xbc, dt, h_q, h_ff, h_fb, h_i, h_g, a_q, a_k, a_v, gate], axis=1)
    return packed.astype(BF16)


def _pad_lane(v):
    v = v.reshape(1, -1)
    return jnp.pad(v, ((0, 0), (0, LANE - v.shape[1])))


def _rope_tables(n):
    pos = jnp.arange(n)
    row = (pos // GRID_W).astype(F32)
    col = (pos % GRID_W).astype(F32)
    half = A_HD // 4
    inv = ROPE_BASE ** (-jnp.arange(0, 2 * half, 2, dtype=F32) / (2 * half))
    ar = row[:, None] * inv
    ac = col[:, None] * inv
    cr, sr, cc, sc = jnp.cos(ar), jnp.sin(ar), jnp.cos(ac), jnp.sin(ac)
    cos = jnp.concatenate([cr, cr, cc, cc] * (LANE // A_HD), axis=1)
    sin = jnp.concatenate([-sr, sr, -sc, sc] * (LANE // A_HD), axis=1)
    return cos, sin


def _tri(n, rev):
    i = jnp.arange(n)
    m = (i[None, :] >= i[:, None]) if rev else (i[None, :] <= i[:, None])
    return m.astype(BF16)


def _pick_tile(seq, want):
    return want if seq % want == 0 else seq


def kernel(x_prompt, x_sample, norm_mix_pre, w_in, m_conv_w, m_conv_b, m_dt_bias, m_a_log, m_d, m_norm_w, h_lb_logits, h_norm_w, a_q_norm, a_k_norm, w_branch, w_out, norm_mix_post, norm_ffn_pre, f_w_up, f_conv_w, f_conv_b, f_w_down, norm_ffn_post):
    sm = jax.nn.softmax(h_lb_logits.astype(F32), axis=0)
    tail = jnp.concatenate([jnp.zeros_like(sm[:1]), sm[1:]], axis=0)
    lower = jnp.cumsum(tail, axis=0)
    one_minus = sm[0:1] + (jnp.sum(tail, axis=0, keepdims=True) - lower)
    lower_floor = jnp.maximum(lower, LB_FLOOR)

    layers = []
    for l in range(DEPTH):
        score_bound = A_HD ** 0.5 * jnp.max(jnp.abs(a_q_norm[l])) * jnp.max(jnp.abs(a_k_norm[l]))
        bounded = score_bound <= ATT_SCORE_LIMIT
        layers.append(dict(
            bounded=bounded.astype(jnp.int32).reshape(1),
            nw_pre=norm_mix_pre[l].reshape(1, D_MODEL),
            w_in=_pack_w_in(w_in[l]),
            cw=m_conv_w[l], cb=m_conv_b[l].reshape(1, M_XBC),
            dtb=_pad_lane(m_dt_bias[l]), a_neg=_pad_lane(-jnp.exp(m_a_log[l].astype(F32)) * LOG2E),
            dskip=jnp.repeat(m_d[l], M_INNER // M_HEADS).reshape(1, M_INNER),
            m_nw=m_norm_w[l].reshape(1, M_INNER),
            lbf=lower_floor[l], om=one_minus[l],
            h_nw=h_norm_w[l].reshape(1, LANE),
            qw=jnp.tile(a_q_norm[l] * (A_HD ** -0.5 * LOG2E), LANE // A_HD).reshape(1, LANE),
            kw=jnp.tile(a_k_norm[l], LANE // A_HD).reshape(1, LANE),
            wb=w_branch[l].astype(BF16), wo=w_out[l].astype(BF16),
            nw_post=norm_mix_post[l].reshape(1, D_MODEL),
            nw_ffn=norm_ffn_pre[l].reshape(1, D_MODEL),
            wup=f_w_up[l].astype(BF16), fcw=f_conv_w[l], fcb=f_conv_b[l].reshape(1, 2 * D_FF),
            wdn=f_w_down[l].astype(BF16),
            nw_ffn_post=norm_ffn_post[l].reshape(1, D_MODEL),
        ))
    tri_ssd = (_tri(SSD_CHUNK, False), _tri(SSD_CHUNK, True))
    tri_hgrn = (_tri(HGRN_CHUNK, False), _tri(HGRN_CHUNK, True))
    head_of_lane = jnp.arange(M_INNER) // (M_INNER // M_HEADS)
    expand_ssd = tuple(
        (jnp.arange(LANE)[:, None] == d * M_HEADS + head_of_lane[None, :]).astype(BF16) for d in range(2))

    def run(x):
        nb, seq, _ = x.shape
        xf = x.reshape(nb * seq, D_MODEL)
        cos, sin = _rope_tables(seq)
        tm = _pick_tile(seq, 512)
        tq = _pick_tile(seq, 512)
        tk = _pick_tile(seq, 512)
        for p in layers:
            z, xbc, dtraw, hin, qh, kh, vh, gates = _inproj(
                xf, p["nw_pre"], p["w_in"], cos, sin, p["qw"], p["kw"], seq, tm)
            y_f, xc = _ssd_fwd(nb, seq, xbc, dtraw, p["cw"], p["cb"], p["dtb"], p["a_neg"],
                               tri_ssd[0], expand_ssd[0], p["dskip"])
            y_m = _ssd_rev(nb, seq, xc, dtraw, p["dtb"], p["a_neg"], tri_ssd[1], expand_ssd[1],
                           y_f, z, p["m_nw"])
            o_f = _hgrn(False, nb, seq, hin, p["lbf"][0:1], p["om"][0:1], tri_hgrn[0], None)
            y_h = _hgrn(True, nb, seq, hin, p["lbf"][1:2], p["om"][1:2], tri_hgrn[1], (o_f, p["h_nw"]))
            y_a = _attn(nb, seq, p["bounded"], qh, kh, vh, tq, tk)
            xf = _merge(xf, y_m, y_h, y_a, gates, p["wb"], p["wo"], p["nw_post"], tm)
            xf = _ffn(xf, p["nw_ffn"], p["wup"], p["fcw"], p["fcb"], p["wdn"], p["nw_ffn_post"], seq, tm)
        return xf.reshape(nb, seq, D_MODEL)

    return (run(x_prompt), run(x_sample))
```

```python
import functools

import jax
import jax.numpy as jnp
from jax import lax
from jax.experimental import pallas as pl
from jax.experimental.pallas import tpu as pltpu

F32 = jnp.float32
BF16 = jnp.bfloat16

D_MODEL = 1024
DEPTH = 4
EPS = 1e-6
LB_FLOOR = 1e-30
GRID_W = 64
ROPE_BASE = 10000.0

M_HEADS = 8
M_INNER = 512
M_STATE = 64
M_CONV = 4
M_XBC = 768
SSD_CHUNK = 128
CHUNKS_PER_STEP = 4

H_HEADS = 4
H_KEY = 128
H_WIDTH = 512
HGRN_CHUNK = 128
HGRN_DIAG = 8

A_HEADS = 8
A_KV = 2
A_HD = 64
A_GROUP = A_HEADS // A_KV

D_FF = 2816
FFN_COLS = 256
FFN_DOWN_GROUP = 4
FFN_HALO = 8
XBC_HALO = 16

LANE = 128
LOG2E = 1.4426950408889634
ATT_SHIFT_LIMIT = 40.0
VMEM_LIMIT = 56 * 1024 * 1024

OFF_Z = 0
OFF_XBC = OFF_Z + M_INNER
OFF_DT = OFF_XBC + M_XBC
OFF_H = OFF_DT + LANE
OFF_Q = OFF_H + 5 * H_WIDTH
OFF_K = OFF_Q + A_HEADS * A_HD
OFF_V = OFF_K + A_KV * A_HD
OFF_G = OFF_V + A_KV * A_HD
N_PACKED = OFF_G + 3 * D_MODEL


def _rms(x, w):
    return x * lax.rsqrt(jnp.mean(x * x, axis=-1, keepdims=True) + EPS) * w


def _sigmoid(x):
    return 1.0 / (1.0 + jnp.exp(-x))


def _dot(a, b):
    return jnp.dot(a, b, preferred_element_type=F32)


def _dot_nt(a, b):
    return lax.dot_general(a, b, (((1,), (1,)), ((), ())), preferred_element_type=F32)


def _split_dot(tri, x):
    hi = x.astype(BF16)
    lo = (x - hi.astype(F32)).astype(BF16)
    return _dot(tri, hi) + _dot(tri, lo)


def _params(**kw):
    return pltpu.CompilerParams(vmem_limit_bytes=VMEM_LIMIT, **kw)


def _inproj_kernel(x_ref, nw_ref, w_ref, cos_ref, sin_ref, qw_ref, kw_ref, qadd_ref, kadd_ref,
                   oz, oxbc, odt, oh, oq, ok, ovt, og):
    h = _rms(x_ref[...], nw_ref[...]).astype(BF16)

    def mm(c0, n):
        return _dot(h, w_ref[:, c0:c0 + n])

    def plain(o_ref, c0, width, step):
        for j in range(0, width, step):
            o_ref[:, j:j + step] = mm(c0 + j, step).astype(o_ref.dtype)

    plain(oz, OFF_Z, M_INNER, 512)
    plain(oxbc, OFF_XBC, M_XBC, 256)
    odt[...] = mm(OFF_DT, LANE)
    plain(oh, OFF_H, 5 * H_WIDTH, 512)

    cos = cos_ref[...]
    sin = sin_ref[...]
    lane = lax.broadcasted_iota(jnp.int32, cos.shape, 1)
    first_half = (lane & 16) == 0
    low = lane < A_HD

    def split_pair(y, add):
        first = jnp.where(low, y, 0.0) + add
        second = jnp.where(low, pltpu.roll(y, A_HD, 1), 0.0) + add
        return first, second

    def norm_rope(y, w):
        sq = y * y
        ss_lo = jnp.sum(jnp.where(low, sq, 0.0), axis=-1, keepdims=True)
        ss_hi = jnp.sum(jnp.where(low, 0.0, sq), axis=-1, keepdims=True)
        yn = y * lax.rsqrt(jnp.where(low, ss_lo, ss_hi) * (1.0 / A_HD) + EPS) * w
        partner = jnp.where(first_half, pltpu.roll(yn, LANE - 16, 1), pltpu.roll(yn, 16, 1))
        return yn * cos + partner * sin

    qw = qw_ref[...]
    qadd = qadd_ref[...]
    kv_add = kadd_ref[...]
    yq = mm(OFF_Q, A_HEADS * A_HD)
    for pr in range(A_HEADS // 2):
        a, b = split_pair(norm_rope(yq[:, pr * LANE:(pr + 1) * LANE], qw), qadd)
        oq[:, 2 * pr * LANE:(2 * pr + 1) * LANE] = a.astype(BF16)
        oq[:, (2 * pr + 1) * LANE:(2 * pr + 2) * LANE] = b.astype(BF16)
    ykv = mm(OFF_K, 2 * A_KV * A_HD)
    a, b = split_pair(norm_rope(ykv[:, :LANE], kw_ref[...]), kv_add)
    ok[:, :LANE] = a.astype(BF16)
    ok[:, LANE:] = b.astype(BF16)
    a, b = split_pair(ykv[:, LANE:], kv_add)
    ovt[:LANE, :] = a.T.astype(BF16)
    ovt[LANE:, :] = b.T.astype(BF16)
    plain(og, OFF_G, 3 * D_MODEL, 512)


def _inproj(x, nw, w, cos, sin, qw, kw, qadd, kadd, seq, tm):
    t = x.shape[0]
    tiles_per_seq = seq // tm
    row = lambda i: (i, 0)
    const = lambda i: (0, 0)
    pos = lambda i: (i % tiles_per_seq, 0)
    outs = ((M_INNER, BF16), (M_XBC, BF16), (LANE, F32), (5 * H_WIDTH, BF16), (A_HEADS * LANE, BF16),
            (A_KV * LANE, BF16), None, (3 * D_MODEL, BF16))
    out_specs = [pl.BlockSpec((A_KV * LANE, tm), lambda i: (0, i)) if o is None
                 else pl.BlockSpec((tm, o[0]), row) for o in outs]
    out_shape = [jax.ShapeDtypeStruct((A_KV * LANE, t), BF16) if o is None
                 else jax.ShapeDtypeStruct((t, o[0]), o[1]) for o in outs]
    return pl.pallas_call(
        _inproj_kernel,
        grid=(t // tm,),
        in_specs=[
            pl.BlockSpec((tm, D_MODEL), row),
            pl.BlockSpec((1, D_MODEL), const),
            pl.BlockSpec((D_MODEL, N_PACKED), const, pipeline_mode=pl.Buffered(1)),
            pl.BlockSpec((tm, LANE), pos),
            pl.BlockSpec((tm, LANE), pos),
            pl.BlockSpec((1, LANE), const),
            pl.BlockSpec((1, LANE), const),
            pl.BlockSpec((1, LANE), const),
            pl.BlockSpec((1, LANE), const),
        ],
        out_specs=out_specs,
        out_shape=out_shape,
        compiler_params=_params(dimension_semantics=("parallel",)),
        name="inproj",
    )(x, nw, w, cos, sin, qw, kw, qadd, kadd)


def _ssd_kernel(rev, nblk, sub, *refs):
    if rev:
        (xc_ref, dt_ref, dtb_ref, a_ref, tri_ref, e_ref, yin_ref, z_ref, nw_ref, o_ref, s_ref) = refs
    else:
        (xp_ref, xm_ref, xn_ref, dt_ref, cw_ref, cb_ref, dtb_ref, a_ref, tri_ref, e_ref,
         dsk_ref, o_ref, oxc_ref, s_ref, xc_s) = refs
    c = pl.program_id(1)
    q = SSD_CHUNK

    @pl.when(c == 0)
    def _():
        s_ref[...] = jnp.zeros_like(s_ref)

    if not rev:
        xp = jnp.where(c > 0, xp_ref[...].astype(F32), 0.0)
        xn = jnp.where(c < nblk - 1, xn_ref[...].astype(F32), 0.0)
        xe = jnp.concatenate([xp, xm_ref[...].astype(F32), xn], axis=0)
        cw = cw_ref[...]
        y = cb_ref[...]
        for k in range(M_CONV):
            start = XBC_HALO - M_CONV // 2 + k
            y = y + xe[start:start + sub * q] * cw[k:k + 1]
        xc_all = y * _sigmoid(y)
        xc_s[...] = xc_all
        oxc_ref[...] = xc_all.astype(oxc_ref.dtype)

    li = lax.broadcasted_iota(jnp.int32, (q, q), 0)
    si = lax.broadcasted_iota(jnp.int32, (q, q), 1)
    mask = (si >= li) if rev else (si <= li)
    lane_s = lax.broadcasted_iota(jnp.int32, (q, LANE), 1)
    lane_x = lax.broadcasted_iota(jnp.int32, (q, 2 * LANE), 1)
    base = M_HEADS if rev else 0
    hpg = M_HEADS // 2

    def chunk(j, carry):
        r0 = pl.multiple_of(((sub - 1 - j) if rev else j) * q, q)
        rows = pl.ds(r0, q)
        xc = xc_ref[rows, :].astype(F32) if rev else xc_s[rows, :]
        xs = xc[:, :M_INNER]
        bm = xc[:, M_INNER:M_INNER + LANE]
        cm = xc[:, M_INNER + LANE:]

        dtr = dt_ref[rows, :] + dtb_ref[...]
        dt = jnp.maximum(dtr, 0.0) + jnp.log1p(jnp.exp(-jnp.abs(dtr)))
        a = dt * a_ref[...]
        u = _split_dot(tri_ref[...], a)
        wide = _expand_heads(jnp.concatenate([u, dt], axis=0), e_ref[...])
        u_w = wide[:q]
        dt_w = wide[q:]
        u_end_w = u_w[0:1] if rev else u_w[q - 1:q]
        off_w = jnp.exp2(u_w)
        w_state_w = dt_w * jnp.exp2(u_end_w - u_w)
        cd_w = jnp.exp2(u_end_w)
        u_t = u.T
        dt_t = dt.T

        ys = []
        for g in range(2):
            sl = slice(g * 2 * LANE, (g + 1) * 2 * LANE)
            gsel = (lane_s >= M_STATE) if g else (lane_s < M_STATE)
            cg = jnp.where(gsel, cm, 0.0).astype(BF16)
            bg = jnp.where(gsel, bm, 0.0).astype(BF16)
            cbm = _dot_nt(cg, bm.astype(BF16))
            xg = xs[:, sl]
            lhs = []
            for r in range(hpg):
                ln = base + g * hpg + r
                dec = jnp.exp2(jnp.where(mask, u[:, ln:ln + 1] - u_t[ln:ln + 1, :], -1e30))
                lhs.append((cbm * dec * dt_t[ln:ln + 1, :]).astype(BF16))
            xr = [jnp.where((lane_x >> 6) == r, xg, 0.0).astype(BF16) for r in range(hpg)]
            y_diag = _dot(jnp.concatenate(lhs, axis=1), jnp.concatenate(xr, axis=0))
            sg = s_ref[g]
            ys.append(y_diag + _dot(cg, sg.astype(BF16)) * off_w[:, sl])
            st = lax.dot_general(bg, (xg * w_state_w[:, sl]).astype(BF16),
                                 (((0,), (0,)), ((), ())), preferred_element_type=F32)
            s_ref[g] = sg * cd_w[:, sl] + st
        yc = jnp.concatenate(ys, axis=1)
        if not rev:
            o_ref[rows, :] = yc + xs * dsk_ref[...]
        else:
            yt = yin_ref[rows, :] + yc
            zf = z_ref[rows, :].astype(F32)
            yt = yt * (zf * _sigmoid(zf))
            nw = nw_ref[...]
            for g in range(2):
                sl = slice(g * 2 * LANE, (g + 1) * 2 * LANE)
                blk = yt[:, sl]
                ms = jnp.mean(blk * blk, axis=-1, keepdims=True)
                o_ref[rows, sl] = (blk * lax.rsqrt(ms + EPS) * nw[:, sl]).astype(o_ref.dtype)
        return carry

    lax.fori_loop(0, sub, chunk, 0, unroll=True)


def _expand_heads(x, e):
    hi = x.astype(BF16)
    rest = x - hi.astype(F32)
    mid = rest.astype(BF16)
    lo = (rest - mid.astype(F32)).astype(BF16)
    return _dot(hi, e) + _dot(mid, e) + _dot(lo, e)


def _chunks_per_step(n_chunks, want):
    while n_chunks % want:
        want -= 1
    return want


def _ssd_fwd(nb, seq, xbc, dtraw, cw, cb, dtb, a_neg, tri, expand, dskip):
    q = SSD_CHUNK
    sub = _chunks_per_step(seq // q, CHUNKS_PER_STEP)
    rows = sub * q
    nblk = seq // rows
    t = nb * seq
    hb = rows // XBC_HALO
    last_halo = t // XBC_HALO - 1
    main = lambda b, c: (b * nblk + c, 0)
    prev = lambda b, c: (jnp.maximum((b * nblk + c) * hb - 1, 0), 0)
    nxt = lambda b, c: (jnp.minimum((b * nblk + c) * hb + hb, last_halo), 0)
    const = lambda b, c: (0, 0)
    return pl.pallas_call(
        functools.partial(_ssd_kernel, False, nblk, sub),
        grid=(nb, nblk),
        in_specs=[
            pl.BlockSpec((XBC_HALO, M_XBC), prev),
            pl.BlockSpec((rows, M_XBC), main),
            pl.BlockSpec((XBC_HALO, M_XBC), nxt),
            pl.BlockSpec((rows, LANE), main),
            pl.BlockSpec((M_CONV, M_XBC), const),
            pl.BlockSpec((1, M_XBC), const),
            pl.BlockSpec((1, LANE), const),
            pl.BlockSpec((1, LANE), const),
            pl.BlockSpec((q, q), const),
            pl.BlockSpec((LANE, M_INNER), const),
            pl.BlockSpec((1, M_INNER), const),
        ],
        out_specs=[pl.BlockSpec((rows, M_INNER), main), pl.BlockSpec((rows, M_XBC), main)],
        out_shape=[jax.ShapeDtypeStruct((t, M_INNER), F32), jax.ShapeDtypeStruct((t, M_XBC), BF16)],
        scratch_shapes=[pltpu.VMEM((2, LANE, 2 * LANE), F32), pltpu.VMEM((rows, M_XBC), F32)],
        compiler_params=_params(dimension_semantics=("arbitrary", "arbitrary")),
        name="ssd_fwd",
    )(xbc, xbc, xbc, dtraw, cw, cb, dtb, a_neg, tri, expand, dskip)


def _ssd_rev(nb, seq, xc, dtraw, dtb, a_neg, tri, expand, y_in, z, nw):
    q = SSD_CHUNK
    sub = _chunks_per_step(seq // q, CHUNKS_PER_STEP)
    rows = sub * q
    nblk = seq // rows
    t = nb * seq
    main = lambda b, c: (b * nblk + nblk - 1 - c, 0)
    const = lambda b, c: (0, 0)
    return pl.pallas_call(
        functools.partial(_ssd_kernel, True, nblk, sub),
        grid=(nb, nblk),
        in_specs=[
            pl.BlockSpec((rows, M_XBC), main),
            pl.BlockSpec((rows, LANE), main),
            pl.BlockSpec((1, LANE), const),
            pl.BlockSpec((1, LANE), const),
            pl.BlockSpec((q, q), const),
            pl.BlockSpec((LANE, M_INNER), const),
            pl.BlockSpec((rows, M_INNER), main),
            pl.BlockSpec((rows, M_INNER), main),
            pl.BlockSpec((1, M_INNER), const),
        ],
        out_specs=pl.BlockSpec((rows, M_INNER), main),
        out_shape=jax.ShapeDtypeStruct((t, M_INNER), BF16),
        scratch_shapes=[pltpu.VMEM((2, LANE, 2 * LANE), F32)],
        compiler_params=_params(dimension_semantics=("arbitrary", "arbitrary")),
        name="ssd_rev",
    )(xc, dtraw, dtb, a_neg, tri, expand, y_in, z, nw)


def _hgrn_kernel(rev, sub, *refs):
    if rev:
        (q_ref, f_ref, i_ref, lbf_ref, om_ref, tri_ref, yin_ref, g_ref, nw_ref,
         o_ref, st_ref, k_s, u_s) = refs
    else:
        (q_ref, f_ref, i_ref, lbf_ref, om_ref, tri_ref, o_ref, st_ref, k_s, u_s) = refs
    c = pl.program_id(1)
    n = HGRN_CHUNK

    @pl.when(c == 0)
    def _():
        st_ref[...] = jnp.zeros_like(st_ref)

    li = lax.broadcasted_iota(jnp.int32, (n, n), 0)
    si = lax.broadcasted_iota(jnp.int32, (n, n), 1)
    level_masks = []
    m = HGRN_DIAG
    while 2 * m <= n:
        sh = m.bit_length() - 1
        same_pair = (li >> (sh + 1)) == (si >> (sh + 1))
        q_half = ((li >> sh) & 1) == (0 if rev else 1)
        k_half = ((si >> sh) & 1) == (1 if rev else 0)
        level_masks.append((m, same_pair & q_half & k_half))
        m *= 2
    rowi = lax.broadcasted_iota(jnp.int32, (HGRN_DIAG, LANE), 0)
    lanei = lax.broadcasted_iota(jnp.int32, (HGRN_DIAG, LANE), 1)
    diag_cols = []
    for b in range(n // HGRN_DIAG):
        rel = lanei - b * HGRN_DIAG
        causal = (rowi <= rel) if rev else (rowi >= rel)
        diag_cols.append(jnp.where(causal, rel, -1))

    def chunk(j, carry):
        r0 = pl.multiple_of(((sub - 1 - j) if rev else j) * n, n)
        rows = pl.ds(r0, n)
        raw = f_ref[rows, :].astype(F32)
        t = jnp.exp(-jnp.abs(raw))
        r = 1.0 / (1.0 + t)
        nonneg = raw >= 0.0
        sig = jnp.where(nonneg, r, t * r)
        sig_neg = jnp.where(nonneg, t * r, r)
        om = om_ref[...]
        logf = jnp.log(lbf_ref[...] + om * sig)
        kk = om * sig_neg
        u = _split_dot(tri_ref[...], logf * LOG2E)
        qf = q_ref[rows, :].astype(F32) * (H_KEY ** -0.5)
        v = i_ref[rows, :].astype(F32)
        k_s[...] = kk
        u_s[...] = u

        u_end = u[0:1] if rev else u[n - 1:n]
        qb = (qf * jnp.exp2(u)).astype(BF16)
        kw = (kk * jnp.exp2(u_end - u)).astype(BF16)
        cd = jnp.exp2(u_end)

        levels = []
        for m, msk in level_masks:
            refs_rows = []
            for p in range(n // (2 * m)):
                ridx = 2 * m * p + (m if rev else m - 1)
                refs_rows.append(jnp.broadcast_to(u[ridx:ridx + 1], (2 * m, H_WIDTH)))
            rb = refs_rows[0] if len(refs_rows) == 1 else jnp.concatenate(refs_rows, axis=0)
            qt = (qf * jnp.exp2(u - rb)).astype(BF16)
            kt = (kk * jnp.exp2(rb - u)).astype(BF16)
            levels.append((qt, kt, msk))

        def diag_scores(h):
            sl = slice(h * LANE, (h + 1) * LANE)
            pieces = []
            for b in range(n // HGRN_DIAG):
                b0 = b * HGRN_DIAG
                q_blk = qf[b0:b0 + HGRN_DIAG, sl]
                u_blk = u[b0:b0 + HGRN_DIAG, sl]
                a_blk = jnp.zeros((HGRN_DIAG, LANE), F32)
                for s in range(HGRN_DIAG):
                    k_row = k_s[b0 + s:b0 + s + 1, sl]
                    u_row = u_s[b0 + s:b0 + s + 1, sl]
                    w = q_blk * (k_row * jnp.exp2(u_blk - u_row))
                    att = jnp.sum(w, axis=-1, keepdims=True)
                    a_blk = jnp.where(diag_cols[b] == s, att, a_blk)
                pieces.append(a_blk)
            return jnp.concatenate(pieces, axis=0)

        for h in range(H_HEADS):
            sl = slice(h * LANE, (h + 1) * LANE)
            att = diag_scores(h)
            for qt, kt, msk in levels:
                att = jnp.where(msk, _dot_nt(qt[:, sl], kt[:, sl]), att)
            st = st_ref[h]
            v_t = v[:, sl].T.astype(BF16)
            lhs = jnp.concatenate([att.astype(BF16), qb[:, sl]], axis=1)
            rhs = jnp.concatenate([v_t, st.astype(BF16)], axis=1)
            oh = _dot_nt(lhs, rhs)
            st_ref[h] = st * cd[:, sl] + _dot(v_t, kw[:, sl])
            if not rev:
                o_ref[rows, sl] = oh
            else:
                oh = oh + yin_ref[rows, sl]
                ms = jnp.mean(oh * oh, axis=-1, keepdims=True)
                gf = g_ref[rows, sl].astype(F32)
                o_ref[rows, sl] = (oh * lax.rsqrt(ms + EPS) * nw_ref[...]
                                   * (gf * _sigmoid(gf))).astype(o_ref.dtype)
        return carry

    lax.fori_loop(0, sub, chunk, 0, unroll=2)


def _hgrn(rev, nb, seq, hin, lbf, om, tri, extra):
    sub = _chunks_per_step(seq // HGRN_CHUNK, CHUNKS_PER_STEP)
    n = sub * HGRN_CHUNK
    nc = seq // n
    t = nb * seq

    def blk(col):
        return lambda b, c: (b * nc + (nc - 1 - c if rev else c), col)

    const = lambda b, c: (0, 0)
    in_specs = [
        pl.BlockSpec((n, H_WIDTH), blk(0)),
        pl.BlockSpec((n, H_WIDTH), blk(2 if rev else 1)),
        pl.BlockSpec((n, H_WIDTH), blk(3)),
        pl.BlockSpec((1, H_WIDTH), const),
        pl.BlockSpec((1, H_WIDTH), const),
        pl.BlockSpec((HGRN_CHUNK, HGRN_CHUNK), const),
    ]
    args = [hin, hin, hin, lbf, om, tri]
    if rev:
        y_in, nw = extra
        in_specs += [pl.BlockSpec((n, H_WIDTH), blk(0)), pl.BlockSpec((n, H_WIDTH), blk(4)),
                     pl.BlockSpec((1, LANE), const)]
        args += [y_in, hin, nw]
        out_dtype = BF16
    else:
        out_dtype = F32
    return pl.pallas_call(
        functools.partial(_hgrn_kernel, rev, sub),
        grid=(nb, nc),
        in_specs=in_specs,
        out_specs=pl.BlockSpec((n, H_WIDTH), blk(0)),
        out_shape=jax.ShapeDtypeStruct((t, H_WIDTH), out_dtype),
        scratch_shapes=[pltpu.VMEM((H_HEADS, LANE, LANE), F32)]
        + [pltpu.VMEM((HGRN_CHUNK, H_WIDTH), F32) for _ in range(2)],
        compiler_params=_params(dimension_semantics=("arbitrary", "arbitrary")),
        name="hgrn_rev" if rev else "hgrn_fwd",
    )(*args)


def _attn_kernel(tq, tk, bounded_ref, q_ref, k_ref, vt_ref, o_ref, m_s, acc_s):
    nk = k_ref.shape[0] // tk
    for g in range(A_KV):
        gl = slice(g * LANE, (g + 1) * LANE)
        qs = jnp.concatenate(
            [q_ref[:, (g * A_GROUP + r) * LANE:(g * A_GROUP + r + 1) * LANE] for r in range(A_GROUP)], axis=0)
        acc_s[...] = jnp.zeros_like(acc_s)

        def tiles(kt):
            k0 = pl.multiple_of(kt * tk, tk)
            return k_ref[pl.ds(k0, tk), gl], vt_ref[gl, pl.ds(k0, tk)]

        def shifted_body(kt, carry):
            kb, vtb = tiles(kt)
            acc_s[...] += _dot(vtb, jnp.exp2(_dot_nt(kb, qs)).astype(BF16))
            return carry

        def online_body(kt, carry):
            kb, vtb = tiles(kt)
            s = _dot_nt(kb, qs)
            m_prev = m_s[...]
            m_new = jnp.maximum(m_prev, jnp.max(s, axis=0, keepdims=True))
            p = jnp.exp2(s - m_new)
            acc_s[...] = jnp.exp2(m_prev - m_new) * acc_s[...] + _dot(vtb, p.astype(BF16))
            m_s[...] = m_new
            return carry

        @pl.when(bounded_ref[0] != 0)
        def _():
            lax.fori_loop(0, nk, shifted_body, 0, unroll=min(4, nk))

        @pl.when(bounded_ref[0] == 0)
        def _():
            m_s[...] = jnp.full_like(m_s, -jnp.inf)
            lax.fori_loop(0, nk, online_body, 0)

        o = acc_s[...]
        o = o[:A_HD] / o[A_HD:A_HD + 1]
        for pr in range(A_GROUP // 2):
            pair = jnp.concatenate([o[:, (2 * pr) * tq:(2 * pr + 1) * tq],
                                    o[:, (2 * pr + 1) * tq:(2 * pr + 2) * tq]], axis=0)
            c0 = g * A_GROUP * A_HD + pr * LANE
            o_ref[:, c0:c0 + LANE] = pair.T.astype(o_ref.dtype)


def _attn(nb, seq, bounded, qh, kh, vt, tq, tk):
    t = nb * seq
    nq = seq // tq
    cols = A_GROUP * tq
    return pl.pallas_call(
        functools.partial(_attn_kernel, tq, tk),
        grid=(nb, nq),
        in_specs=[
            pl.BlockSpec(memory_space=pltpu.SMEM),
            pl.BlockSpec((tq, A_HEADS * LANE), lambda b, i: (b * nq + i, 0)),
            pl.BlockSpec((seq, A_KV * LANE), lambda b, i: (b, 0)),
            pl.BlockSpec((A_KV * LANE, seq), lambda b, i: (0, b)),
        ],
        out_specs=pl.BlockSpec((tq, A_HEADS * A_HD), lambda b, i: (b * nq + i, 0)),
        out_shape=jax.ShapeDtypeStruct((t, A_HEADS * A_HD), BF16),
        scratch_shapes=[pltpu.VMEM((1, cols), F32), pltpu.VMEM((LANE, cols), F32)],
        compiler_params=_params(dimension_semantics=("parallel", "parallel")),
        name="attn",
    )(bounded, qh, kh, vt)


def _merge_kernel(x_ref, ym_ref, yh_ref, ya_ref, g_ref, wb_ref, wo_ref, nw_ref, o_ref):
    mixed = None
    for i, y_ref in enumerate((ym_ref, yh_ref, ya_ref)):
        proj = _dot(y_ref[...], wb_ref[i])
        gate = _sigmoid(g_ref[:, i * D_MODEL:(i + 1) * D_MODEL].astype(F32))
        mixed = gate * proj if mixed is None else mixed + gate * proj
    out = _dot(mixed.astype(BF16), wo_ref[...])
    o_ref[...] = x_ref[...] + _rms(out, nw_ref[...])


def _merge(x, ym, yh, ya, gates, wb, wo, nw, tm):
    t = x.shape[0]
    row = lambda i: (i, 0)
    return pl.pallas_call(
        _merge_kernel,
        grid=(t // tm,),
        in_specs=[
            pl.BlockSpec((tm, D_MODEL), row),
            pl.BlockSpec((tm, M_INNER), row),
            pl.BlockSpec((tm, H_WIDTH), row),
            pl.BlockSpec((tm, A_HEADS * A_HD), row),
            pl.BlockSpec((tm, 3 * D_MODEL), row),
            pl.BlockSpec((3, M_INNER, D_MODEL), lambda i: (0, 0, 0)),
            pl.BlockSpec((D_MODEL, D_MODEL), lambda i: (0, 0)),
            pl.BlockSpec((1, D_MODEL), lambda i: (0, 0)),
        ],
        out_specs=pl.BlockSpec((tm, D_MODEL), row),
        out_shape=jax.ShapeDtypeStruct((t, D_MODEL), F32),
        compiler_params=_params(dimension_semantics=("parallel",)),
        name="merge",
    )(x, ym, yh, ya, gates, wb, wo, nw)


def _ffn_kernel(tiles_per_seq, xp_ref, xm_ref, xn_ref, nw1_ref, wup_ref, cw_ref, cb_ref, wdn_ref,
                nw2_ref, o_ref, act_s, *u_s):
    j = pl.program_id(0) % tiles_per_seq
    tm = xm_ref.shape[0]
    x = xm_ref[...]
    nw1 = nw1_ref[...]
    hp = jnp.where(j > 0, _rms(xp_ref[...], nw1), 0.0)
    hn = jnp.where(j < tiles_per_seq - 1, _rms(xn_ref[...], nw1), 0.0)
    hext = jnp.concatenate([hp, _rms(x, nw1), hn], axis=0).astype(BF16)

    def up(slot, c0):
        u_s[slot][...] = _dot(hext, wup_ref[:, c0:c0 + FFN_COLS])

    def conv(slot, c0):
        w = cw_ref[:, c0:c0 + FFN_COLS]
        out = cb_ref[:, c0:c0 + FFN_COLS]
        for k in range(3):
            out = out + u_s[slot][pl.ds(FFN_HALO - 1 + k, tm), :] * w[k:k + 1]
        return out

    n_chunks = D_FF // FFN_COLS
    acc = None
    group_start = 0
    up(0, 0)
    up(1, D_FF)
    for cc in range(n_chunks):
        c0 = cc * FFN_COLS
        par = 2 * (cc % 2)
        if cc + 1 < n_chunks:
            up(2 - par, c0 + FFN_COLS)
            up(3 - par, D_FF + c0 + FFN_COLS)
        ug = conv(par, c0)
        uu = conv(par + 1, D_FF + c0)
        act_s[:, c0:c0 + FFN_COLS] = (ug * _sigmoid(ug) * uu).astype(BF16)
        if (cc + 1) % FFN_DOWN_GROUP == 0 or cc + 1 == n_chunks:
            k0, k1 = group_start * FFN_COLS, c0 + FFN_COLS
            part = _dot(act_s[:, k0:k1], wdn_ref[k0:k1, :])
            acc = part if acc is None else acc + part
            group_start = cc + 1
    o_ref[...] = x + _rms(acc, nw2_ref[...])


def _ffn(x, nw1, wup, cw, cb, wdn, nw2, seq, tm):
    t = x.shape[0]
    tiles_per_seq = seq // tm
    hb = tm // FFN_HALO
    last_halo = t // FFN_HALO - 1
    const = lambda i: (0, 0)
    single = pl.Buffered(1)
    return pl.pallas_call(
        functools.partial(_ffn_kernel, tiles_per_seq),
        grid=(t // tm,),
        in_specs=[
            pl.BlockSpec((FFN_HALO, D_MODEL), lambda i: (jnp.maximum(i * hb - 1, 0), 0)),
            pl.BlockSpec((tm, D_MODEL), lambda i: (i, 0)),
            pl.BlockSpec((FFN_HALO, D_MODEL), lambda i: (jnp.minimum(i * hb + hb, last_halo), 0)),
            pl.BlockSpec((1, D_MODEL), const),
            pl.BlockSpec((D_MODEL, 2 * D_FF), const, pipeline_mode=single),
            pl.BlockSpec((3, 2 * D_FF), const),
            pl.BlockSpec((1, 2 * D_FF), const),
            pl.BlockSpec((D_FF, D_MODEL), const, pipeline_mode=single),
            pl.BlockSpec((1, D_MODEL), const),
        ],
        out_specs=pl.BlockSpec((tm, D_MODEL), lambda i: (i, 0)),
        out_shape=jax.ShapeDtypeStruct((t, D_MODEL), F32),
        scratch_shapes=[pltpu.VMEM((tm, D_FF), BF16)]
        + [pltpu.VMEM((tm + 2 * FFN_HALO, FFN_COLS), F32) for _ in range(4)],
        compiler_params=_params(dimension_semantics=("parallel",)),
        name="ffn",
    )(x, x, x, nw1, wup, cw, cb, wdn, nw2)


def _pack_w_in(w):
    sizes = (M_INNER, M_XBC, M_HEADS, M_HEADS, H_WIDTH, H_WIDTH, H_WIDTH, H_WIDTH, H_WIDTH,
             A_HEADS * A_HD, A_KV * A_HD, A_KV * A_HD, 3 * D_MODEL)
    parts, off = [], 0
    for s in sizes:
        parts.append(w[:, off:off + s])
        off += s
    (m_z, m_xbc, m_dtf, m_dtb, h_q, h_ff, h_fb, h_i, h_g, a_q, a_k, a_v, gate) = parts
    dt = jnp.pad(jnp.concatenate([m_dtf, m_dtb], axis=1), ((0, 0), (0, LANE - 2 * M_HEADS)))
    packed = jnp.concatenate(
        [m_z, m_xbc, dt, h_q, h_ff, h_fb, h_i, h_g, a_q, a_k, a_v, gate], axis=1)
    return packed.astype(BF16)


def _pad_lane(v):
    v = v.reshape(1, -1)
    return jnp.pad(v, ((0, 0), (0, LANE - v.shape[1])))


def _rope_tables(n):
    pos = jnp.arange(n)
    row = (pos // GRID_W).astype(F32)
    col = (pos % GRID_W).astype(F32)
    half = A_HD // 4
    inv = ROPE_BASE ** (-jnp.arange(0, 2 * half, 2, dtype=F32) / (2 * half))
    ar = row[:, None] * inv
    ac = col[:, None] * inv
    cr, sr, cc, sc = jnp.cos(ar), jnp.sin(ar), jnp.cos(ac), jnp.sin(ac)
    cos = jnp.concatenate([cr, cr, cc, cc] * (LANE // A_HD), axis=1)
    sin = jnp.concatenate([-sr, sr, -sc, sc] * (LANE // A_HD), axis=1)
    return cos, sin


def _tri(n, rev):
    i = jnp.arange(n)
    m = (i[None, :] >= i[:, None]) if rev else (i[None, :] <= i[:, None])
    return m.astype(BF16)


def _pick_tile(seq, want):
    return want if seq % want == 0 else seq


def kernel(x_prompt, x_sample, norm_mix_pre, w_in, m_conv_w, m_conv_b, m_dt_bias, m_a_log, m_d, m_norm_w, h_lb_logits, h_norm_w, a_q_norm, a_k_norm, w_branch, w_out, norm_mix_post, norm_ffn_pre, f_w_up, f_conv_w, f_conv_b, f_w_down, norm_ffn_post):
    sm = jax.nn.softmax(h_lb_logits.astype(F32), axis=0)
    tail = jnp.concatenate([jnp.zeros_like(sm[:1]), sm[1:]], axis=0)
    lower = jnp.cumsum(tail, axis=0)
    one_minus = sm[0:1] + (jnp.sum(tail, axis=0, keepdims=True) - lower)
    lower_floor = jnp.maximum(lower, LB_FLOOR)

    lane_a = jnp.arange(LANE) == A_HD
    kadd = lane_a.astype(F32).reshape(1, LANE)
    layers = []
    for l in range(DEPTH):
        score_bound = A_HD ** 0.5 * jnp.max(jnp.abs(a_q_norm[l])) * jnp.max(jnp.abs(a_k_norm[l]))
        bounded = score_bound <= ATT_SHIFT_LIMIT
        shift = jnp.where(bounded, score_bound * LOG2E, 0.0)
        layers.append(dict(
            bounded=bounded.astype(jnp.int32).reshape(1),
            qadd=jnp.where(lane_a, -shift, 0.0).astype(F32).reshape(1, LANE), kadd=kadd,
            nw_pre=norm_mix_pre[l].reshape(1, D_MODEL),
            w_in=_pack_w_in(w_in[l]),
            cw=m_conv_w[l], cb=m_conv_b[l].reshape(1, M_XBC),
            dtb=_pad_lane(m_dt_bias[l]), a_neg=_pad_lane(-jnp.exp(m_a_log[l].astype(F32)) * LOG2E),
            dskip=jnp.repeat(m_d[l], M_INNER // M_HEADS).reshape(1, M_INNER),
            m_nw=m_norm_w[l].reshape(1, M_INNER),
            lbf=lower_floor[l], om=one_minus[l],
            h_nw=h_norm_w[l].reshape(1, LANE),
            qw=jnp.tile(a_q_norm[l] * (A_HD ** -0.5 * LOG2E), LANE // A_HD).reshape(1, LANE),
            kw=jnp.tile(a_k_norm[l], LANE // A_HD).reshape(1, LANE),
            wb=w_branch[l].astype(BF16), wo=w_out[l].astype(BF16),
            nw_post=norm_mix_post[l].reshape(1, D_MODEL),
            nw_ffn=norm_ffn_pre[l].reshape(1, D_MODEL),
            wup=f_w_up[l].astype(BF16), fcw=f_conv_w[l], fcb=f_conv_b[l].reshape(1, 2 * D_FF),
            wdn=f_w_down[l].astype(BF16),
            nw_ffn_post=norm_ffn_post[l].reshape(1, D_MODEL),
        ))
    tri_ssd = (_tri(SSD_CHUNK, False), _tri(SSD_CHUNK, True))
    tri_hgrn = (_tri(HGRN_CHUNK, False), _tri(HGRN_CHUNK, True))
    head_of_lane = jnp.arange(M_INNER) // (M_INNER // M_HEADS)
    expand_ssd = tuple(
        (jnp.arange(LANE)[:, None] == d * M_HEADS + head_of_lane[None, :]).astype(BF16) for d in range(2))

    def run(x):
        nb, seq, _ = x.shape
        xf = x.reshape(nb * seq, D_MODEL)
        cos, sin = _rope_tables(seq)
        tm = _pick_tile(seq, 512)
        tq = _pick_tile(seq, 512)
        tk = _pick_tile(seq, 512)
        for p in layers:
            z, xbc, dtraw, hin, qh, kh, vh, gates = _inproj(
                xf, p["nw_pre"], p["w_in"], cos, sin, p["qw"], p["kw"],
                p["qadd"], p["kadd"], seq, tm)
            y_f, xc = _ssd_fwd(nb, seq, xbc, dtraw, p["cw"], p["cb"], p["dtb"], p["a_neg"],
                               tri_ssd[0], expand_ssd[0], p["dskip"])
            y_m = _ssd_rev(nb, seq, xc, dtraw, p["dtb"], p["a_neg"], tri_ssd[1], expand_ssd[1],
                           y_f, z, p["m_nw"])
            o_f = _hgrn(False, nb, seq, hin, p["lbf"][0:1], p["om"][0:1], tri_hgrn[0], None)
            y_h = _hgrn(True, nb, seq, hin, p["lbf"][1:2], p["om"][1:2], tri_hgrn[1], (o_f, p["h_nw"]))
            y_a = _attn(nb, seq, p["bounded"], qh, kh, vh, tq, tk)
            xf = _merge(xf, y_m, y_h, y_a, gates, p["wb"], p["wo"], p["nw_post"], tm)
            xf = _ffn(xf, p["nw_ffn"], p["wup"], p["fcw"], p["fcb"], p["wdn"], p["nw_ffn_post"], seq, tm)
        return xf.reshape(nb, seq, D_MODEL)

    return (run(x_prompt), run(x_sample))
```

```python
import functools

import jax
import jax.numpy as jnp
from jax import lax
from jax.experimental import pallas as pl
from jax.experimental.pallas import tpu as pltpu

F32 = jnp.float32
BF16 = jnp.bfloat16

D_MODEL = 1024
DEPTH = 4
EPS = 1e-6
LB_FLOOR = 1e-30
GRID_W = 64
ROPE_BASE = 10000.0

M_HEADS = 8
M_INNER = 512
M_STATE = 64
M_CONV = 4
M_XBC = 768
SSD_CHUNK = 128
CHUNKS_PER_STEP = 8

H_HEADS = 4
H_KEY = 128
H_WIDTH = 512
HGRN_CHUNK = 128
HGRN_DIAG = 8

A_HEADS = 8
A_KV = 2
A_HD = 64
A_GROUP = A_HEADS // A_KV

D_FF = 2816
FFN_COLS = 256
FFN_DOWN_GROUP = 4
FFN_HALO = 8
XBC_HALO = 16

LANE = 128
LOG2E = 1.4426950408889634
ATT_SHIFT_LIMIT = 40.0
VMEM_LIMIT = 56 * 1024 * 1024

OFF_Z = 0
OFF_XBC = OFF_Z + M_INNER
OFF_DT = OFF_XBC + M_XBC
OFF_H = OFF_DT + LANE
OFF_Q = OFF_H + 5 * H_WIDTH
OFF_K = OFF_Q + A_HEADS * A_HD
OFF_V = OFF_K + A_KV * A_HD
OFF_G = OFF_V + A_KV * A_HD
N_PACKED = OFF_G + 3 * D_MODEL


def _rms(x, w):
    return x * lax.rsqrt(jnp.mean(x * x, axis=-1, keepdims=True) + EPS) * w


def _sigmoid(x):
    return 1.0 / (1.0 + jnp.exp(-x))


def _dot(a, b):
    return jnp.dot(a, b, preferred_element_type=F32)


def _dot_nt(a, b):
    return lax.dot_general(a, b, (((1,), (1,)), ((), ())), preferred_element_type=F32)


def _split_dot(tri, x):
    hi = x.astype(BF16)
    lo = (x - hi.astype(F32)).astype(BF16)
    return _dot(tri, hi) + _dot(tri, lo)


def _params(**kw):
    return pltpu.CompilerParams(vmem_limit_bytes=VMEM_LIMIT, **kw)


def _inproj_kernel(x_ref, nw_ref, w_ref, cos_ref, sin_ref, qw_ref, kw_ref, qadd_ref, kadd_ref,
                   oz, oxbc, odt, oh, oq, ok, ovt, og):
    h = _rms(x_ref[...], nw_ref[...]).astype(BF16)

    def mm(c0, n):
        return _dot(h, w_ref[:, c0:c0 + n])

    def plain(o_ref, c0, width, step):
        for j in range(0, width, step):
            o_ref[:, j:j + step] = mm(c0 + j, step).astype(o_ref.dtype)

    plain(oz, OFF_Z, M_INNER, 512)
    plain(oxbc, OFF_XBC, M_XBC, 256)
    odt[...] = mm(OFF_DT, LANE)
    plain(oh, OFF_H, 5 * H_WIDTH, 512)

    cos = cos_ref[...]
    sin = sin_ref[...]
    lane = lax.broadcasted_iota(jnp.int32, cos.shape, 1)
    first_half = (lane & 16) == 0
    low = lane < A_HD

    def split_pair(y, add):
        first = jnp.where(low, y, 0.0) + add
        second = jnp.where(low, pltpu.roll(y, A_HD, 1), 0.0) + add
        return first, second

    def norm_rope(y, w):
        sq = y * y
        ss_lo = jnp.sum(jnp.where(low, sq, 0.0), axis=-1, keepdims=True)
        ss_hi = jnp.sum(jnp.where(low, 0.0, sq), axis=-1, keepdims=True)
        yn = y * lax.rsqrt(jnp.where(low, ss_lo, ss_hi) * (1.0 / A_HD) + EPS) * w
        partner = jnp.where(first_half, pltpu.roll(yn, LANE - 16, 1), pltpu.roll(yn, 16, 1))
        return yn * cos + partner * sin

    qw = qw_ref[...]
    qadd = qadd_ref[...]
    kv_add = kadd_ref[...]
    yq = mm(OFF_Q, A_HEADS * A_HD)
    for pr in range(A_HEADS // 2):
        a, b = split_pair(norm_rope(yq[:, pr * LANE:(pr + 1) * LANE], qw), qadd)
        oq[:, 2 * pr * LANE:(2 * pr + 1) * LANE] = a.astype(BF16)
        oq[:, (2 * pr + 1) * LANE:(2 * pr + 2) * LANE] = b.astype(BF16)
    ykv = mm(OFF_K, 2 * A_KV * A_HD)
    a, b = split_pair(norm_rope(ykv[:, :LANE], kw_ref[...]), kv_add)
    ok[:, :LANE] = a.astype(BF16)
    ok[:, LANE:] = b.astype(BF16)
    a, b = split_pair(ykv[:, LANE:], kv_add)
    ovt[:LANE, :] = a.T.astype(BF16)
    ovt[LANE:, :] = b.T.astype(BF16)
    plain(og, OFF_G, 3 * D_MODEL, 512)


def _inproj(x, nw, w, cos, sin, qw, kw, qadd, kadd, seq, tm):
    t = x.shape[0]
    tiles_per_seq = seq // tm
    row = lambda i: (i, 0)
    const = lambda i: (0, 0)
    pos = lambda i: (i % tiles_per_seq, 0)
    outs = ((M_INNER, BF16), (M_XBC, BF16), (LANE, F32), (5 * H_WIDTH, BF16), (A_HEADS * LANE, BF16),
            (A_KV * LANE, BF16), None, (3 * D_MODEL, BF16))
    out_specs = [pl.BlockSpec((A_KV * LANE, tm), lambda i: (0, i)) if o is None
                 else pl.BlockSpec((tm, o[0]), row) for o in outs]
    out_shape = [jax.ShapeDtypeStruct((A_KV * LANE, t), BF16) if o is None
                 else jax.ShapeDtypeStruct((t, o[0]), o[1]) for o in outs]
    return pl.pallas_call(
        _inproj_kernel,
        grid=(t // tm,),
        in_specs=[
            pl.BlockSpec((tm, D_MODEL), row),
            pl.BlockSpec((1, D_MODEL), const),
            pl.BlockSpec((D_MODEL, N_PACKED), const, pipeline_mode=pl.Buffered(1)),
            pl.BlockSpec((tm, LANE), pos),
            pl.BlockSpec((tm, LANE), pos),
            pl.BlockSpec((1, LANE), const),
            pl.BlockSpec((1, LANE), const),
            pl.BlockSpec((1, LANE), const),
            pl.BlockSpec((1, LANE), const),
        ],
        out_specs=out_specs,
        out_shape=out_shape,
        compiler_params=_params(dimension_semantics=("parallel",)),
        name="inproj",
    )(x, nw, w, cos, sin, qw, kw, qadd, kadd)


def _ssd_kernel(rev, nblk, sub, *refs):
    if rev:
        (xc_ref, dt_ref, dtb_ref, a_ref, tri_ref, e_ref, yin_ref, z_ref, nw_ref, o_ref, s_ref) = refs
    else:
        (xp_ref, xm_ref, xn_ref, dt_ref, cw_ref, cb_ref, dtb_ref, a_ref, tri_ref, e_ref,
         dsk_ref, o_ref, oxc_ref, s_ref, xc_s) = refs
    c = pl.program_id(1)
    q = SSD_CHUNK

    @pl.when(c == 0)
    def _():
        s_ref[...] = jnp.zeros_like(s_ref)

    if not rev:
        xp = jnp.where(c > 0, xp_ref[...].astype(F32), 0.0)
        xn = jnp.where(c < nblk - 1, xn_ref[...].astype(F32), 0.0)
        xe = jnp.concatenate([xp, xm_ref[...].astype(F32), xn], axis=0)
        cw = cw_ref[...]
        y = cb_ref[...]
        for k in range(M_CONV):
            start = XBC_HALO - M_CONV // 2 + k
            y = y + xe[start:start + sub * q] * cw[k:k + 1]
        xc_all = y * _sigmoid(y)
        xc_s[...] = xc_all
        oxc_ref[...] = xc_all.astype(oxc_ref.dtype)

    li = lax.broadcasted_iota(jnp.int32, (q, q), 0)
    si = lax.broadcasted_iota(jnp.int32, (q, q), 1)
    mask = (si >= li) if rev else (si <= li)
    lane_s = lax.broadcasted_iota(jnp.int32, (q, LANE), 1)
    lane_x = lax.broadcasted_iota(jnp.int32, (q, 2 * LANE), 1)
    base = M_HEADS if rev else 0
    hpg = M_HEADS // 2

    def chunk(j, carry):
        r0 = pl.multiple_of(((sub - 1 - j) if rev else j) * q, q)
        rows = pl.ds(r0, q)
        xc = xc_ref[rows, :].astype(F32) if rev else xc_s[rows, :]
        xs = xc[:, :M_INNER]
        bm = xc[:, M_INNER:M_INNER + LANE]
        cm = xc[:, M_INNER + LANE:]

        dtr = dt_ref[rows, :] + dtb_ref[...]
        dt = jnp.maximum(dtr, 0.0) + jnp.log1p(jnp.exp(-jnp.abs(dtr)))
        a = dt * a_ref[...]
        u = _split_dot(tri_ref[...], a)
        wide = _expand_heads(jnp.concatenate([u, dt], axis=0), e_ref[...])
        u_w = wide[:q]
        dt_w = wide[q:]
        u_end_w = u_w[0:1] if rev else u_w[q - 1:q]
        off_w = jnp.exp2(u_w)
        w_state_w = dt_w * jnp.exp2(u_end_w - u_w)
        cd_w = jnp.exp2(u_end_w)
        u_t = u.T
        dt_t = dt.T

        ys = []
        for g in range(2):
            sl = slice(g * 2 * LANE, (g + 1) * 2 * LANE)
            gsel = (lane_s >= M_STATE) if g else (lane_s < M_STATE)
            cg = jnp.where(gsel, cm, 0.0).astype(BF16)
            bg = jnp.where(gsel, bm, 0.0).astype(BF16)
            cbm = _dot_nt(cg, bm.astype(BF16))
            xg = xs[:, sl]
            lhs = []
            for r in range(hpg):
                ln = base + g * hpg + r
                dec = jnp.exp2(jnp.where(mask, u[:, ln:ln + 1] - u_t[ln:ln + 1, :], -1e30))
                lhs.append((cbm * dec * dt_t[ln:ln + 1, :]).astype(BF16))
            xr = [jnp.where((lane_x >> 6) == r, xg, 0.0).astype(BF16) for r in range(hpg)]
            y_diag = _dot(jnp.concatenate(lhs, axis=1), jnp.concatenate(xr, axis=0))
            sg = s_ref[g]
            ys.append(y_diag + _dot(cg, sg.astype(BF16)) * off_w[:, sl])
            st = lax.dot_general(bg, (xg * w_state_w[:, sl]).astype(BF16),
                                 (((0,), (0,)), ((), ())), preferred_element_type=F32)
            s_ref[g] = sg * cd_w[:, sl] + st
        yc = jnp.concatenate(ys, axis=1)
        if not rev:
            o_ref[rows, :] = yc + xs * dsk_ref[...]
        else:
            yt = yin_ref[rows, :] + yc
            zf = z_ref[rows, :].astype(F32)
            yt = yt * (zf * _sigmoid(zf))
            nw = nw_ref[...]
            for g in range(2):
                sl = slice(g * 2 * LANE, (g + 1) * 2 * LANE)
                blk = yt[:, sl]
                ms = jnp.mean(blk * blk, axis=-1, keepdims=True)
                o_ref[rows, sl] = (blk * lax.rsqrt(ms + EPS) * nw[:, sl]).astype(o_ref.dtype)
        return carry

    lax.fori_loop(0, sub, chunk, 0, unroll=True)


def _expand_heads(x, e):
    hi = x.astype(BF16)
    rest = x - hi.astype(F32)
    mid = rest.astype(BF16)
    lo = (rest - mid.astype(F32)).astype(BF16)
    return _dot(hi, e) + _dot(mid, e) + _dot(lo, e)


def _chunks_per_step(n_chunks, want):
    while n_chunks % want:
        want -= 1
    return want


def _ssd_fwd(nb, seq, xbc, dtraw, cw, cb, dtb, a_neg, tri, expand, dskip):
    q = SSD_CHUNK
    sub = _chunks_per_step(seq // q, CHUNKS_PER_STEP)
    rows = sub * q
    nblk = seq // rows
    t = nb * seq
    hb = rows // XBC_HALO
    last_halo = t // XBC_HALO - 1
    main = lambda b, c: (b * nblk + c, 0)
    prev = lambda b, c: (jnp.maximum((b * nblk + c) * hb - 1, 0), 0)
    nxt = lambda b, c: (jnp.minimum((b * nblk + c) * hb + hb, last_halo), 0)
    const = lambda b, c: (0, 0)
    return pl.pallas_call(
        functools.partial(_ssd_kernel, False, nblk, sub),
        grid=(nb, nblk),
        in_specs=[
            pl.BlockSpec((XBC_HALO, M_XBC), prev),
            pl.BlockSpec((rows, M_XBC), main),
            pl.BlockSpec((XBC_HALO, M_XBC), nxt),
            pl.BlockSpec((rows, LANE), main),
            pl.BlockSpec((M_CONV, M_XBC), const),
            pl.BlockSpec((1, M_XBC), const),
            pl.BlockSpec((1, LANE), const),
            pl.BlockSpec((1, LANE), const),
            pl.BlockSpec((q, q), const),
            pl.BlockSpec((LANE, M_INNER), const),
            pl.BlockSpec((1, M_INNER), const),
        ],
        out_specs=[pl.BlockSpec((rows, M_INNER), main), pl.BlockSpec((rows, M_XBC), main)],
        out_shape=[jax.ShapeDtypeStruct((t, M_INNER), F32), jax.ShapeDtypeStruct((t, M_XBC), BF16)],
        scratch_shapes=[pltpu.VMEM((2, LANE, 2 * LANE), F32), pltpu.VMEM((rows, M_XBC), F32)],
        compiler_params=_params(dimension_semantics=("arbitrary", "arbitrary")),
        name="ssd_fwd",
    )(xbc, xbc, xbc, dtraw, cw, cb, dtb, a_neg, tri, expand, dskip)


def _ssd_rev(nb, seq, xc, dtraw, dtb, a_neg, tri, expand, y_in, z, nw):
    q = SSD_CHUNK
    sub = _chunks_per_step(seq // q, CHUNKS_PER_STEP)
    rows = sub * q
    nblk = seq // rows
    t = nb * seq
    main = lambda b, c: (b * nblk + nblk - 1 - c, 0)
    const = lambda b, c: (0, 0)
    return pl.pallas_call(
        functools.partial(_ssd_kernel, True, nblk, sub),
        grid=(nb, nblk),
        in_specs=[
            pl.BlockSpec((rows, M_XBC), main),
            pl.BlockSpec((rows, LANE), main),
            pl.BlockSpec((1, LANE), const),
            pl.BlockSpec((1, LANE), const),
            pl.BlockSpec((q, q), const),
            pl.BlockSpec((LANE, M_INNER), const),
            pl.BlockSpec((rows, M_INNER), main),
            pl.BlockSpec((rows, M_INNER), main),
            pl.BlockSpec((1, M_INNER), const),
        ],
        out_specs=pl.BlockSpec((rows, M_INNER), main),
        out_shape=jax.ShapeDtypeStruct((t, M_INNER), BF16),
        scratch_shapes=[pltpu.VMEM((2, LANE, 2 * LANE), F32)],
        compiler_params=_params(dimension_semantics=("arbitrary", "arbitrary")),
        name="ssd_rev",
    )(xc, dtraw, dtb, a_neg, tri, expand, y_in, z, nw)


def _hgrn_kernel(rev, sub, *refs):
    if rev:
        (q_ref, f_ref, i_ref, lbf_ref, om_ref, tri_ref, yin_ref, g_ref, nw_ref,
         o_ref, st_ref, k_s, u_s) = refs
    else:
        (q_ref, f_ref, i_ref, lbf_ref, om_ref, tri_ref, o_ref, st_ref, k_s, u_s) = refs
    c = pl.program_id(1)
    n = HGRN_CHUNK

    @pl.when(c == 0)
    def _():
        st_ref[...] = jnp.zeros_like(st_ref)

    li = lax.broadcasted_iota(jnp.int32, (n, n), 0)
    si = lax.broadcasted_iota(jnp.int32, (n, n), 1)
    level_masks = []
    m = HGRN_DIAG
    while 2 * m <= n:
        sh = m.bit_length() - 1
        same_pair = (li >> (sh + 1)) == (si >> (sh + 1))
        q_half = ((li >> sh) & 1) == (0 if rev else 1)
        k_half = ((si >> sh) & 1) == (1 if rev else 0)
        level_masks.append((m, same_pair & q_half & k_half))
        m *= 2
    rowi = lax.broadcasted_iota(jnp.int32, (HGRN_DIAG, LANE), 0)
    lanei = lax.broadcasted_iota(jnp.int32, (HGRN_DIAG, LANE), 1)
    diag_cols = []
    for b in range(n // HGRN_DIAG):
        rel = lanei - b * HGRN_DIAG
        causal = (rowi <= rel) if rev else (rowi >= rel)
        diag_cols.append(jnp.where(causal, rel, -1))

    def chunk(j, carry):
        r0 = pl.multiple_of(((sub - 1 - j) if rev else j) * n, n)
        rows = pl.ds(r0, n)
        raw = f_ref[rows, :].astype(F32)
        t = jnp.exp(-jnp.abs(raw))
        r = 1.0 / (1.0 + t)
        nonneg = raw >= 0.0
        sig = jnp.where(nonneg, r, t * r)
        sig_neg = jnp.where(nonneg, t * r, r)
        om = om_ref[...]
        logf = jnp.log(lbf_ref[...] + om * sig)
        kk = om * sig_neg
        u = _split_dot(tri_ref[...], logf * LOG2E)
        qf = q_ref[rows, :].astype(F32) * (H_KEY ** -0.5)
        v = i_ref[rows, :].astype(F32)
        k_s[...] = kk
        u_s[...] = u

        u_end = u[0:1] if rev else u[n - 1:n]
        qb = (qf * jnp.exp2(u)).astype(BF16)
        kw = (kk * jnp.exp2(u_end - u)).astype(BF16)
        cd = jnp.exp2(u_end)

        levels = []
        for m, msk in level_masks:
            refs_rows = []
            for p in range(n // (2 * m)):
                ridx = 2 * m * p + (m if rev else m - 1)
                refs_rows.append(jnp.broadcast_to(u[ridx:ridx + 1], (2 * m, H_WIDTH)))
            rb = refs_rows[0] if len(refs_rows) == 1 else jnp.concatenate(refs_rows, axis=0)
            qt = (qf * jnp.exp2(u - rb)).astype(BF16)
            kt = (kk * jnp.exp2(rb - u)).astype(BF16)
            levels.append((qt, kt, msk))

        def diag_scores(h):
            sl = slice(h * LANE, (h + 1) * LANE)
            pieces = []
            for b in range(n // HGRN_DIAG):
                b0 = b * HGRN_DIAG
                q_blk = qf[b0:b0 + HGRN_DIAG, sl]
                u_blk = u[b0:b0 + HGRN_DIAG, sl]
                a_blk = jnp.zeros((HGRN_DIAG, LANE), F32)
                for s in range(HGRN_DIAG):
                    k_row = k_s[b0 + s:b0 + s + 1, sl]
                    u_row = u_s[b0 + s:b0 + s + 1, sl]
                    w = q_blk * (k_row * jnp.exp2(u_blk - u_row))
                    att = jnp.sum(w, axis=-1, keepdims=True)
                    a_blk = jnp.where(diag_cols[b] == s, att, a_blk)
                pieces.append(a_blk)
            return jnp.concatenate(pieces, axis=0)

        for h in range(H_HEADS):
            sl = slice(h * LANE, (h + 1) * LANE)
            att = diag_scores(h)
            for qt, kt, msk in levels:
                att = jnp.where(msk, _dot_nt(qt[:, sl], kt[:, sl]), att)
            st = st_ref[h]
            v_t = v[:, sl].T.astype(BF16)
            lhs = jnp.concatenate([att.astype(BF16), qb[:, sl]], axis=1)
            rhs = jnp.concatenate([v_t, st.astype(BF16)], axis=1)
            oh = _dot_nt(lhs, rhs)
            st_ref[h] = st * cd[:, sl] + _dot(v_t, kw[:, sl])
            if not rev:
                o_ref[rows, sl] = oh
            else:
                oh = oh + yin_ref[rows, sl]
                ms = jnp.mean(oh * oh, axis=-1, keepdims=True)
                gf = g_ref[rows, sl].astype(F32)
                o_ref[rows, sl] = (oh * lax.rsqrt(ms + EPS) * nw_ref[...]
                                   * (gf * _sigmoid(gf))).astype(o_ref.dtype)
        return carry

    lax.fori_loop(0, sub, chunk, 0, unroll=2)


def _hgrn(rev, nb, seq, hin, lbf, om, tri, extra):
    sub = _chunks_per_step(seq // HGRN_CHUNK, CHUNKS_PER_STEP)
    n = sub * HGRN_CHUNK
    nc = seq // n
    t = nb * seq

    def blk(col):
        return lambda b, c: (b * nc + (nc - 1 - c if rev else c), col)

    const = lambda b, c: (0, 0)
    in_specs = [
        pl.BlockSpec((n, H_WIDTH), blk(0)),
        pl.BlockSpec((n, H_WIDTH), blk(2 if rev else 1)),
        pl.BlockSpec((n, H_WIDTH), blk(3)),
        pl.BlockSpec((1, H_WIDTH), const),
        pl.BlockSpec((1, H_WIDTH), const),
        pl.BlockSpec((HGRN_CHUNK, HGRN_CHUNK), const),
    ]
    args = [hin, hin, hin, lbf, om, tri]
    if rev:
        y_in, nw = extra
        in_specs += [pl.BlockSpec((n, H_WIDTH), blk(0)), pl.BlockSpec((n, H_WIDTH), blk(4)),
                     pl.BlockSpec((1, LANE), const)]
        args += [y_in, hin, nw]
        out_dtype = BF16
    else:
        out_dtype = F32
    return pl.pallas_call(
        functools.partial(_hgrn_kernel, rev, sub),
        grid=(nb, nc),
        in_specs=in_specs,
        out_specs=pl.BlockSpec((n, H_WIDTH), blk(0)),
        out_shape=jax.ShapeDtypeStruct((t, H_WIDTH), out_dtype),
        scratch_shapes=[pltpu.VMEM((H_HEADS, LANE, LANE), F32)]
        + [pltpu.VMEM((HGRN_CHUNK, H_WIDTH), F32) for _ in range(2)],
        compiler_params=_params(dimension_semantics=("arbitrary", "arbitrary")),
        name="hgrn_rev" if rev else "hgrn_fwd",
    )(*args)


def _attn_kernel(tq, tk, bounded_ref, q_ref, k_ref, vt_ref, o_ref, m_s, acc_s):
    nk = k_ref.shape[0] // tk
    for g in range(A_KV):
        gl = slice(g * LANE, (g + 1) * LANE)
        qs = jnp.concatenate(
            [q_ref[:, (g * A_GROUP + r) * LANE:(g * A_GROUP + r + 1) * LANE] for r in range(A_GROUP)], axis=0)
        acc_s[...] = jnp.zeros_like(acc_s)

        def tiles(kt):
            k0 = pl.multiple_of(kt * tk, tk)
            return k_ref[pl.ds(k0, tk), gl], vt_ref[gl, pl.ds(k0, tk)]

        def shifted_body(kt, carry):
            kb, vtb = tiles(kt)
            acc_s[...] += _dot(vtb, jnp.exp2(_dot_nt(kb, qs)).astype(BF16))
            return carry

        def online_body(kt, carry):
            kb, vtb = tiles(kt)
            s = _dot_nt(kb, qs)
            m_prev = m_s[...]
            m_new = jnp.maximum(m_prev, jnp.max(s, axis=0, keepdims=True))
            p = jnp.exp2(s - m_new)
            acc_s[...] = jnp.exp2(m_prev - m_new) * acc_s[...] + _dot(vtb, p.astype(BF16))
            m_s[...] = m_new
            return carry

        @pl.when(bounded_ref[0] != 0)
        def _():
            lax.fori_loop(0, nk, shifted_body, 0, unroll=min(4, nk))

        @pl.when(bounded_ref[0] == 0)
        def _():
            m_s[...] = jnp.full_like(m_s, -jnp.inf)
            lax.fori_loop(0, nk, online_body, 0)

        o = acc_s[...]
        o = o[:A_HD] / o[A_HD:A_HD + 1]
        for pr in range(A_GROUP // 2):
            pair = jnp.concatenate([o[:, (2 * pr) * tq:(2 * pr + 1) * tq],
                                    o[:, (2 * pr + 1) * tq:(2 * pr + 2) * tq]], axis=0)
            c0 = g * A_GROUP * A_HD + pr * LANE
            o_ref[:, c0:c0 + LANE] = pair.T.astype(o_ref.dtype)


def _attn(nb, seq, bounded, qh, kh, vt, tq, tk):
    t = nb * seq
    nq = seq // tq
    cols = A_GROUP * tq
    return pl.pallas_call(
        functools.partial(_attn_kernel, tq, tk),
        grid=(nb, nq),
        in_specs=[
            pl.BlockSpec(memory_space=pltpu.SMEM),
            pl.BlockSpec((tq, A_HEADS * LANE), lambda b, i: (b * nq + i, 0)),
            pl.BlockSpec((seq, A_KV * LANE), lambda b, i: (b, 0)),
            pl.BlockSpec((A_KV * LANE, seq), lambda b, i: (0, b)),
        ],
        out_specs=pl.BlockSpec((tq, A_HEADS * A_HD), lambda b, i: (b * nq + i, 0)),
        out_shape=jax.ShapeDtypeStruct((t, A_HEADS * A_HD), BF16),
        scratch_shapes=[pltpu.VMEM((1, cols), F32), pltpu.VMEM((LANE, cols), F32)],
        compiler_params=_params(dimension_semantics=("parallel", "parallel")),
        name="attn",
    )(bounded, qh, kh, vt)


def _merge_kernel(x_ref, ym_ref, yh_ref, ya_ref, g_ref, wb_ref, wo_ref, nw_ref, o_ref):
    mixed = None
    for i, y_ref in enumerate((ym_ref, yh_ref, ya_ref)):
        proj = _dot(y_ref[...], wb_ref[i])
        gate = _sigmoid(g_ref[:, i * D_MODEL:(i + 1) * D_MODEL].astype(F32))
        mixed = gate * proj if mixed is None else mixed + gate * proj
    out = _dot(mixed.astype(BF16), wo_ref[...])
    o_ref[...] = x_ref[...] + _rms(out, nw_ref[...])


def _merge(x, ym, yh, ya, gates, wb, wo, nw, tm):
    t = x.shape[0]
    row = lambda i: (i, 0)
    return pl.pallas_call(
        _merge_kernel,
        grid=(t // tm,),
        in_specs=[
            pl.BlockSpec((tm, D_MODEL), row),
            pl.BlockSpec((tm, M_INNER), row),
            pl.BlockSpec((tm, H_WIDTH), row),
            pl.BlockSpec((tm, A_HEADS * A_HD), row),
            pl.BlockSpec((tm, 3 * D_MODEL), row),
            pl.BlockSpec((3, M_INNER, D_MODEL), lambda i: (0, 0, 0)),
            pl.BlockSpec((D_MODEL, D_MODEL), lambda i: (0, 0)),
            pl.BlockSpec((1, D_MODEL), lambda i: (0, 0)),
        ],
        out_specs=pl.BlockSpec((tm, D_MODEL), row),
        out_shape=jax.ShapeDtypeStruct((t, D_MODEL), F32),
        compiler_params=_params(dimension_semantics=("parallel",)),
        name="merge",
    )(x, ym, yh, ya, gates, wb, wo, nw)


def _ffn_kernel(tiles_per_seq, xp_ref, xm_ref, xn_ref, nw1_ref, wup_ref, cw_ref, cb_ref, wdn_ref,
                nw2_ref, o_ref, act_s, *u_s):
    j = pl.program_id(0) % tiles_per_seq
    tm = xm_ref.shape[0]
    x = xm_ref[...]
    nw1 = nw1_ref[...]
    hp = jnp.where(j > 0, _rms(xp_ref[...], nw1), 0.0)
    hn = jnp.where(j < tiles_per_seq - 1, _rms(xn_ref[...], nw1), 0.0)
    hext = jnp.concatenate([hp, _rms(x, nw1), hn], axis=0).astype(BF16)

    def up(slot, c0):
        u_s[slot][...] = _dot(hext, wup_ref[:, c0:c0 + FFN_COLS])

    def conv(slot, c0):
        w = cw_ref[:, c0:c0 + FFN_COLS]
        out = cb_ref[:, c0:c0 + FFN_COLS]
        for k in range(3):
            out = out + u_s[slot][pl.ds(FFN_HALO - 1 + k, tm), :] * w[k:k + 1]
        return out

    n_chunks = D_FF // FFN_COLS
    acc = None
    group_start = 0
    up(0, 0)
    up(1, D_FF)
    for cc in range(n_chunks):
        c0 = cc * FFN_COLS
        par = 2 * (cc % 2)
        if cc + 1 < n_chunks:
            up(2 - par, c0 + FFN_COLS)
            up(3 - par, D_FF + c0 + FFN_COLS)
        ug = conv(par, c0)
        uu = conv(par + 1, D_FF + c0)
        act_s[:, c0:c0 + FFN_COLS] = (ug * _sigmoid(ug) * uu).astype(BF16)
        if (cc + 1) % FFN_DOWN_GROUP == 0 or cc + 1 == n_chunks:
            k0, k1 = group_start * FFN_COLS, c0 + FFN_COLS
            part = _dot(act_s[:, k0:k1], wdn_ref[k0:k1, :])
            acc = part if acc is None else acc + part
            group_start = cc + 1
    o_ref[...] = x + _rms(acc, nw2_ref[...])


def _ffn(x, nw1, wup, cw, cb, wdn, nw2, seq, tm):
    t = x.shape[0]
    tiles_per_seq = seq // tm
    hb = tm // FFN_HALO
    last_halo = t // FFN_HALO - 1
    const = lambda i: (0, 0)
    single = pl.Buffered(1)
    return pl.pallas_call(
        functools.partial(_ffn_kernel, tiles_per_seq),
        grid=(t // tm,),
        in_specs=[
            pl.BlockSpec((FFN_HALO, D_MODEL), lambda i: (jnp.maximum(i * hb - 1, 0), 0)),
            pl.BlockSpec((tm, D_MODEL), lambda i: (i, 0)),
            pl.BlockSpec((FFN_HALO, D_MODEL), lambda i: (jnp.minimum(i * hb + hb, last_halo), 0)),
            pl.BlockSpec((1, D_MODEL), const),
            pl.BlockSpec((D_MODEL, 2 * D_FF), const, pipeline_mode=single),
            pl.BlockSpec((3, 2 * D_FF), const),
            pl.BlockSpec((1, 2 * D_FF), const),
            pl.BlockSpec((D_FF, D_MODEL), const, pipeline_mode=single),
            pl.BlockSpec((1, D_MODEL), const),
        ],
        out_specs=pl.BlockSpec((tm, D_MODEL), lambda i: (i, 0)),
        out_shape=jax.ShapeDtypeStruct((t, D_MODEL), F32),
        scratch_shapes=[pltpu.VMEM((tm, D_FF), BF16)]
        + [pltpu.VMEM((tm + 2 * FFN_HALO, FFN_COLS), F32) for _ in range(4)],
        compiler_params=_params(dimension_semantics=("parallel",)),
        name="ffn",
    )(x, x, x, nw1, wup, cw, cb, wdn, nw2)


def _pack_w_in(w):
    sizes = (M_INNER, M_XBC, M_HEADS, M_HEADS, H_WIDTH, H_WIDTH, H_WIDTH, H_WIDTH, H_WIDTH,
             A_HEADS * A_HD, A_KV * A_HD, A_KV * A_HD, 3 * D_MODEL)
    parts, off = [], 0
    for s in sizes:
        parts.append(w[:, off:off + s])
        off += s
    (m_z, m_xbc, m_dtf, m_dtb, h_q, h_ff, h_fb, h_i, h_g, a_q, a_k, a_v, gate) = parts
    dt = jnp.pad(jnp.concatenate([m_dtf, m_dtb], axis=1), ((0, 0), (0, LANE - 2 * M_HEADS)))
    packed = jnp.concatenate(
        [m_z, m_xbc, dt, h_q, h_ff, h_fb, h_i, h_g, a_q, a_k, a_v, gate], axis=1)
    return packed.astype(BF16)


def _pad_lane(v):
    v = v.reshape(1, -1)
    return jnp.pad(v, ((0, 0), (0, LANE - v.shape[1])))


def _rope_tables(n):
    pos = jnp.arange(n)
    row = (pos // GRID_W).astype(F32)
    col = (pos % GRID_W).astype(F32)
    half = A_HD // 4
    inv = ROPE_BASE ** (-jnp.arange(0, 2 * half, 2, dtype=F32) / (2 * half))
    ar = row[:, None] * inv
    ac = col[:, None] * inv
    cr, sr, cc, sc = jnp.cos(ar), jnp.sin(ar), jnp.cos(ac), jnp.sin(ac)
    cos = jnp.concatenate([cr, cr, cc, cc] * (LANE // A_HD), axis=1)
    sin = jnp.concatenate([-sr, sr, -sc, sc] * (LANE // A_HD), axis=1)
    return cos, sin


def _tri(n, rev):
    i = jnp.arange(n)
    m = (i[None, :] >= i[:, None]) if rev else (i[None, :] <= i[:, None])
    return m.astype(BF16)


def _pick_tile(seq, want):
    return want if seq % want == 0 else seq


def kernel(x_prompt, x_sample, norm_mix_pre, w_in, m_conv_w, m_conv_b, m_dt_bias, m_a_log, m_d, m_norm_w, h_lb_logits, h_norm_w, a_q_norm, a_k_norm, w_branch, w_out, norm_mix_post, norm_ffn_pre, f_w_up, f_conv_w, f_conv_b, f_w_down, norm_ffn_post):
    sm = jax.nn.softmax(h_lb_logits.astype(F32), axis=0)
    tail = jnp.concatenate([jnp.zeros_like(sm[:1]), sm[1:]], axis=0)
    lower = jnp.cumsum(tail, axis=0)
    one_minus = sm[0:1] + (jnp.sum(tail, axis=0, keepdims=True) - lower)
    lower_floor = jnp.maximum(lower, LB_FLOOR)

    lane_a = jnp.arange(LANE) == A_HD
    kadd = lane_a.astype(F32).reshape(1, LANE)
    layers = []
    for l in range(DEPTH):
        score_bound = A_HD ** 0.5 * jnp.max(jnp.abs(a_q_norm[l])) * jnp.max(jnp.abs(a_k_norm[l]))
        bounded = score_bound <= ATT_SHIFT_LIMIT
        shift = jnp.where(bounded, score_bound * LOG2E, 0.0)
        layers.append(dict(
            bounded=bounded.astype(jnp.int32).reshape(1),
            qadd=jnp.where(lane_a, -shift, 0.0).astype(F32).reshape(1, LANE), kadd=kadd,
            nw_pre=norm_mix_pre[l].reshape(1, D_MODEL),
            w_in=_pack_w_in(w_in[l]),
            cw=m_conv_w[l], cb=m_conv_b[l].reshape(1, M_XBC),
            dtb=_pad_lane(m_dt_bias[l]), a_neg=_pad_lane(-jnp.exp(m_a_log[l].astype(F32)) * LOG2E),
            dskip=jnp.repeat(m_d[l], M_INNER // M_HEADS).reshape(1, M_INNER),
            m_nw=m_norm_w[l].reshape(1, M_INNER),
            lbf=lower_floor[l], om=one_minus[l],
            h_nw=h_norm_w[l].reshape(1, LANE),
            qw=jnp.tile(a_q_norm[l] * (A_HD ** -0.5 * LOG2E), LANE // A_HD).reshape(1, LANE),
            kw=jnp.tile(a_k_norm[l], LANE // A_HD).reshape(1, LANE),
            wb=w_branch[l].astype(BF16), wo=w_out[l].astype(BF16),
            nw_post=norm_mix_post[l].reshape(1, D_MODEL),
            nw_ffn=norm_ffn_pre[l].reshape(1, D_MODEL),
            wup=f_w_up[l].astype(BF16), fcw=f_conv_w[l], fcb=f_conv_b[l].reshape(1, 2 * D_FF),
            wdn=f_w_down[l].astype(BF16),
            nw_ffn_post=norm_ffn_post[l].reshape(1, D_MODEL),
        ))
    tri_ssd = (_tri(SSD_CHUNK, False), _tri(SSD_CHUNK, True))
    tri_hgrn = (_tri(HGRN_CHUNK, False), _tri(HGRN_CHUNK, True))
    head_of_lane = jnp.arange(M_INNER) // (M_INNER // M_HEADS)
    expand_ssd = tuple(
        (jnp.arange(LANE)[:, None] == d * M_HEADS + head_of_lane[None, :]).astype(BF16) for d in range(2))

    def run(x):
        nb, seq, _ = x.shape
        xf = x.reshape(nb * seq, D_MODEL)
        cos, sin = _rope_tables(seq)
        tm = _pick_tile(seq, 512)
        tq = _pick_tile(seq, 512)
        tk = _pick_tile(seq, 512)
        for p in layers:
            z, xbc, dtraw, hin, qh, kh, vh, gates = _inproj(
                xf, p["nw_pre"], p["w_in"], cos, sin, p["qw"], p["kw"],
                p["qadd"], p["kadd"], seq, tm)
            y_f, xc = _ssd_fwd(nb, seq, xbc, dtraw, p["cw"], p["cb"], p["dtb"], p["a_neg"],
                               tri_ssd[0], expand_ssd[0], p["dskip"])
            y_m = _ssd_rev(nb, seq, xc, dtraw, p["dtb"], p["a_neg"], tri_ssd[1], expand_ssd[1],
                           y_f, z, p["m_nw"])
            o_f = _hgrn(False, nb, seq, hin, p["lbf"][0:1], p["om"][0:1], tri_hgrn[0], None)
            y_h = _hgrn(True, nb, seq, hin, p["lbf"][1:2], p["om"][1:2], tri_hgrn[1], (o_f, p["h_nw"]))
            y_a = _attn(nb, seq, p["bounded"], qh, kh, vh, tq, tk)
            xf = _merge(xf, y_m, y_h, y_a, gates, p["wb"], p["wo"], p["nw_post"], tm)
            xf = _ffn(xf, p["nw_ffn"], p["wup"], p["fcw"], p["fcb"], p["wdn"], p["nw_ffn_post"], seq, tm)
        return xf.reshape(nb, seq, D_MODEL)

    return (run(x_prompt), run(x_sample))
```

```python
import functools

import jax
import jax.numpy as jnp
from jax import lax
from jax.experimental import pallas as pl
from jax.experimental.pallas import tpu as pltpu

F32 = jnp.float32
BF16 = jnp.bfloat16

D_MODEL = 1024
DEPTH = 4
EPS = 1e-6
LB_FLOOR = 1e-30
GRID_W = 64
ROPE_BASE = 10000.0

M_HEADS = 8
M_INNER = 512
M_STATE = 64
M_CONV = 4
M_XBC = 768
SSD_CHUNK = 128
CHUNKS_PER_STEP = 8

H_HEADS = 4
H_KEY = 128
H_WIDTH = 512
HGRN_CHUNK = 128
HGRN_DIAG = 8

A_HEADS = 8
A_KV = 2
A_HD = 64
A_GROUP = A_HEADS // A_KV

D_FF = 2816
FFN_COLS = 256
FFN_DOWN_GROUP = 11
FFN_HALO = 8
XBC_HALO = 16

LANE = 128
LOG2E = 1.4426950408889634
ATT_SHIFT_LIMIT = 40.0
VMEM_LIMIT = 56 * 1024 * 1024

OFF_Z = 0
OFF_XBC = OFF_Z + M_INNER
OFF_DT = OFF_XBC + M_XBC
OFF_H = OFF_DT + LANE
OFF_Q = OFF_H + 5 * H_WIDTH
OFF_K = OFF_Q + A_HEADS * A_HD
OFF_V = OFF_K + A_KV * A_HD
OFF_G = OFF_V + A_KV * A_HD
N_PACKED = OFF_G + 3 * D_MODEL


def _rms(x, w):
    return x * lax.rsqrt(jnp.mean(x * x, axis=-1, keepdims=True) + EPS) * w


def _sigmoid(x):
    return 1.0 / (1.0 + jnp.exp(-x))


def _dot(a, b):
    return jnp.dot(a, b, preferred_element_type=F32)


def _dot_nt(a, b):
    return lax.dot_general(a, b, (((1,), (1,)), ((), ())), preferred_element_type=F32)


def _split_dot(tri, x):
    hi = x.astype(BF16)
    lo = (x - hi.astype(F32)).astype(BF16)
    return _dot(tri, hi) + _dot(tri, lo)


def _params(**kw):
    return pltpu.CompilerParams(vmem_limit_bytes=VMEM_LIMIT, **kw)


def _inproj_kernel(x_ref, nw_ref, w_ref, cos_ref, sin_ref, qw_ref, kw_ref, qadd_ref, kadd_ref,
                   oz, oxbc, odt, oh, oq, ok, ovt, og):
    h = _rms(x_ref[...], nw_ref[...]).astype(BF16)

    def mm(c0, n):
        return _dot(h, w_ref[:, c0:c0 + n])

    def plain(o_ref, c0, width, step):
        for j in range(0, width, step):
            o_ref[:, j:j + step] = mm(c0 + j, step).astype(o_ref.dtype)

    plain(oz, OFF_Z, M_INNER, 512)
    plain(oxbc, OFF_XBC, M_XBC, 256)
    odt[...] = mm(OFF_DT, LANE)
    plain(oh, OFF_H, 5 * H_WIDTH, 512)

    cos = cos_ref[...]
    sin = sin_ref[...]
    lane = lax.broadcasted_iota(jnp.int32, cos.shape, 1)
    first_half = (lane & 16) == 0
    low = lane < A_HD

    def split_pair(y, add):
        first = jnp.where(low, y, 0.0) + add
        second = jnp.where(low, pltpu.roll(y, A_HD, 1), 0.0) + add
        return first, second

    def norm_rope(y, w):
        sq = y * y
        ss_lo = jnp.sum(jnp.where(low, sq, 0.0), axis=-1, keepdims=True)
        ss_hi = jnp.sum(jnp.where(low, 0.0, sq), axis=-1, keepdims=True)
        yn = y * lax.rsqrt(jnp.where(low, ss_lo, ss_hi) * (1.0 / A_HD) + EPS) * w
        partner = jnp.where(first_half, pltpu.roll(yn, LANE - 16, 1), pltpu.roll(yn, 16, 1))
        return yn * cos + partner * sin

    qw = qw_ref[...]
    qadd = qadd_ref[...]
    kv_add = kadd_ref[...]
    yq = mm(OFF_Q, A_HEADS * A_HD)
    for pr in range(A_HEADS // 2):
        a, b = split_pair(norm_rope(yq[:, pr * LANE:(pr + 1) * LANE], qw), qadd)
        oq[:, 2 * pr * LANE:(2 * pr + 1) * LANE] = a.astype(BF16)
        oq[:, (2 * pr + 1) * LANE:(2 * pr + 2) * LANE] = b.astype(BF16)
    ykv = mm(OFF_K, 2 * A_KV * A_HD)
    a, b = split_pair(norm_rope(ykv[:, :LANE], kw_ref[...]), kv_add)
    ok[:, :LANE] = a.astype(BF16)
    ok[:, LANE:] = b.astype(BF16)
    a, b = split_pair(ykv[:, LANE:], kv_add)
    ovt[:LANE, :] = a.T.astype(BF16)
    ovt[LANE:, :] = b.T.astype(BF16)
    plain(og, OFF_G, 3 * D_MODEL, 512)


def _inproj(x, nw, w, cos, sin, qw, kw, qadd, kadd, seq, tm):
    t = x.shape[0]
    tiles_per_seq = seq // tm
    row = lambda i: (i, 0)
    const = lambda i: (0, 0)
    pos = lambda i: (i % tiles_per_seq, 0)
    outs = ((M_INNER, BF16), (M_XBC, BF16), (LANE, F32), (5 * H_WIDTH, BF16), (A_HEADS * LANE, BF16),
            (A_KV * LANE, BF16), None, (3 * D_MODEL, BF16))
    out_specs = [pl.BlockSpec((A_KV * LANE, tm), lambda i: (0, i)) if o is None
                 else pl.BlockSpec((tm, o[0]), row) for o in outs]
    out_shape = [jax.ShapeDtypeStruct((A_KV * LANE, t), BF16) if o is None
                 else jax.ShapeDtypeStruct((t, o[0]), o[1]) for o in outs]
    return pl.pallas_call(
        _inproj_kernel,
        grid=(t // tm,),
        in_specs=[
            pl.BlockSpec((tm, D_MODEL), row),
            pl.BlockSpec((1, D_MODEL), const),
            pl.BlockSpec((D_MODEL, N_PACKED), const, pipeline_mode=pl.Buffered(1)),
            pl.BlockSpec((tm, LANE), pos),
            pl.BlockSpec((tm, LANE), pos),
            pl.BlockSpec((1, LANE), const),
            pl.BlockSpec((1, LANE), const),
            pl.BlockSpec((1, LANE), const),
            pl.BlockSpec((1, LANE), const),
        ],
        out_specs=out_specs,
        out_shape=out_shape,
        compiler_params=_params(dimension_semantics=("parallel",)),
        name="inproj",
    )(x, nw, w, cos, sin, qw, kw, qadd, kadd)


def _ssd_kernel(rev, nblk, sub, *refs):
    if rev:
        (xc_ref, dt_ref, dtb_ref, a_ref, tri_ref, e_ref, yin_ref, z_ref, nw_ref, o_ref, s_ref) = refs
    else:
        (xp_ref, xm_ref, xn_ref, dt_ref, cw_ref, cb_ref, dtb_ref, a_ref, tri_ref, e_ref,
         dsk_ref, o_ref, oxc_ref, s_ref, xc_s) = refs
    c = pl.program_id(1)
    q = SSD_CHUNK

    @pl.when(c == 0)
    def _():
        s_ref[...] = jnp.zeros_like(s_ref)

    if not rev:
        xp = jnp.where(c > 0, xp_ref[...].astype(F32), 0.0)
        xn = jnp.where(c < nblk - 1, xn_ref[...].astype(F32), 0.0)
        xe = jnp.concatenate([xp, xm_ref[...].astype(F32), xn], axis=0)
        cw = cw_ref[...]
        y = cb_ref[...]
        for k in range(M_CONV):
            start = XBC_HALO - M_CONV // 2 + k
            y = y + xe[start:start + sub * q] * cw[k:k + 1]
        xc_all = y * _sigmoid(y)
        xc_s[...] = xc_all
        oxc_ref[...] = xc_all.astype(oxc_ref.dtype)

    li = lax.broadcasted_iota(jnp.int32, (q, q), 0)
    si = lax.broadcasted_iota(jnp.int32, (q, q), 1)
    mask = (si >= li) if rev else (si <= li)
    lane_s = lax.broadcasted_iota(jnp.int32, (q, LANE), 1)
    lane_x = lax.broadcasted_iota(jnp.int32, (q, 2 * LANE), 1)
    base = M_HEADS if rev else 0
    hpg = M_HEADS // 2

    def chunk(j, carry):
        r0 = pl.multiple_of(((sub - 1 - j) if rev else j) * q, q)
        rows = pl.ds(r0, q)
        xc = xc_ref[rows, :].astype(F32) if rev else xc_s[rows, :]
        xs = xc[:, :M_INNER]
        bm = xc[:, M_INNER:M_INNER + LANE]
        cm = xc[:, M_INNER + LANE:]

        dtr = dt_ref[rows, :] + dtb_ref[...]
        dt = jnp.maximum(dtr, 0.0) + jnp.log1p(jnp.exp(-jnp.abs(dtr)))
        a = dt * a_ref[...]
        u = _split_dot(tri_ref[...], a)
        wide = _expand_heads(jnp.concatenate([u, dt], axis=0), e_ref[...])
        u_w = wide[:q]
        dt_w = wide[q:]
        u_end_w = u_w[0:1] if rev else u_w[q - 1:q]
        off_w = jnp.exp2(u_w)
        w_state_w = dt_w * jnp.exp2(u_end_w - u_w)
        cd_w = jnp.exp2(u_end_w)
        u_t = u.T
        dt_t = dt.T

        ys = []
        for g in range(2):
            sl = slice(g * 2 * LANE, (g + 1) * 2 * LANE)
            gsel = (lane_s >= M_STATE) if g else (lane_s < M_STATE)
            cg = jnp.where(gsel, cm, 0.0).astype(BF16)
            bg = jnp.where(gsel, bm, 0.0).astype(BF16)
            cbm = _dot_nt(cg, bm.astype(BF16))
            xg = xs[:, sl]
            lhs = []
            for r in range(hpg):
                ln = base + g * hpg + r
                dec = jnp.exp2(jnp.where(mask, u[:, ln:ln + 1] - u_t[ln:ln + 1, :], -1e30))
                lhs.append((cbm * dec * dt_t[ln:ln + 1, :]).astype(BF16))
            xr = [jnp.where((lane_x >> 6) == r, xg, 0.0).astype(BF16) for r in range(hpg)]
            y_diag = _dot(jnp.concatenate(lhs, axis=1), jnp.concatenate(xr, axis=0))
            sg = s_ref[g]
            ys.append(y_diag + _dot(cg, sg.astype(BF16)) * off_w[:, sl])
            st = lax.dot_general(bg, (xg * w_state_w[:, sl]).astype(BF16),
                                 (((0,), (0,)), ((), ())), preferred_element_type=F32)
            s_ref[g] = sg * cd_w[:, sl] + st
        yc = jnp.concatenate(ys, axis=1)
        if not rev:
            o_ref[rows, :] = yc + xs * dsk_ref[...]
        else:
            yt = yin_ref[rows, :] + yc
            zf = z_ref[rows, :].astype(F32)
            yt = yt * (zf * _sigmoid(zf))
            nw = nw_ref[...]
            for g in range(2):
                sl = slice(g * 2 * LANE, (g + 1) * 2 * LANE)
                blk = yt[:, sl]
                ms = jnp.mean(blk * blk, axis=-1, keepdims=True)
                o_ref[rows, sl] = (blk * lax.rsqrt(ms + EPS) * nw[:, sl]).astype(o_ref.dtype)
        return carry

    lax.fori_loop(0, sub, chunk, 0, unroll=True)


def _expand_heads(x, e):
    hi = x.astype(BF16)
    rest = x - hi.astype(F32)
    mid = rest.astype(BF16)
    lo = (rest - mid.astype(F32)).astype(BF16)
    return _dot(hi, e) + _dot(mid, e) + _dot(lo, e)


def _chunks_per_step(n_chunks, want):
    while n_chunks % want:
        want -= 1
    return want


def _ssd_fwd(nb, seq, xbc, dtraw, cw, cb, dtb, a_neg, tri, expand, dskip):
    q = SSD_CHUNK
    sub = _chunks_per_step(seq // q, CHUNKS_PER_STEP)
    rows = sub * q
    nblk = seq // rows
    t = nb * seq
    hb = rows // XBC_HALO
    last_halo = t // XBC_HALO - 1
    main = lambda b, c: (b * nblk + c, 0)
    prev = lambda b, c: (jnp.maximum((b * nblk + c) * hb - 1, 0), 0)
    nxt = lambda b, c: (jnp.minimum((b * nblk + c) * hb + hb, last_halo), 0)
    const = lambda b, c: (0, 0)
    return pl.pallas_call(
        functools.partial(_ssd_kernel, False, nblk, sub),
        grid=(nb, nblk),
        in_specs=[
            pl.BlockSpec((XBC_HALO, M_XBC), prev),
            pl.BlockSpec((rows, M_XBC), main),
            pl.BlockSpec((XBC_HALO, M_XBC), nxt),
            pl.BlockSpec((rows, LANE), main),
            pl.BlockSpec((M_CONV, M_XBC), const),
            pl.BlockSpec((1, M_XBC), const),
            pl.BlockSpec((1, LANE), const),
            pl.BlockSpec((1, LANE), const),
            pl.BlockSpec((q, q), const),
            pl.BlockSpec((LANE, M_INNER), const),
            pl.BlockSpec((1, M_INNER), const),
        ],
        out_specs=[pl.BlockSpec((rows, M_INNER), main), pl.BlockSpec((rows, M_XBC), main)],
        out_shape=[jax.ShapeDtypeStruct((t, M_INNER), F32), jax.ShapeDtypeStruct((t, M_XBC), BF16)],
        scratch_shapes=[pltpu.VMEM((2, LANE, 2 * LANE), F32), pltpu.VMEM((rows, M_XBC), F32)],
        compiler_params=_params(dimension_semantics=("arbitrary", "arbitrary")),
        name="ssd_fwd",
    )(xbc, xbc, xbc, dtraw, cw, cb, dtb, a_neg, tri, expand, dskip)


def _ssd_rev(nb, seq, xc, dtraw, dtb, a_neg, tri, expand, y_in, z, nw):
    q = SSD_CHUNK
    sub = _chunks_per_step(seq // q, CHUNKS_PER_STEP)
    rows = sub * q
    nblk = seq // rows
    t = nb * seq
    main = lambda b, c: (b * nblk + nblk - 1 - c, 0)
    const = lambda b, c: (0, 0)
    return pl.pallas_call(
        functools.partial(_ssd_kernel, True, nblk, sub),
        grid=(nb, nblk),
        in_specs=[
            pl.BlockSpec((rows, M_XBC), main),
            pl.BlockSpec((rows, LANE), main),
            pl.BlockSpec((1, LANE), const),
            pl.BlockSpec((1, LANE), const),
            pl.BlockSpec((q, q), const),
            pl.BlockSpec((LANE, M_INNER), const),
            pl.BlockSpec((rows, M_INNER), main),
            pl.BlockSpec((rows, M_INNER), main),
            pl.BlockSpec((1, M_INNER), const),
        ],
        out_specs=pl.BlockSpec((rows, M_INNER), main),
        out_shape=jax.ShapeDtypeStruct((t, M_INNER), BF16),
        scratch_shapes=[pltpu.VMEM((2, LANE, 2 * LANE), F32)],
        compiler_params=_params(dimension_semantics=("arbitrary", "arbitrary")),
        name="ssd_rev",
    )(xc, dtraw, dtb, a_neg, tri, expand, y_in, z, nw)


def _hgrn_kernel(rev, sub, *refs):
    if rev:
        (q_ref, f_ref, i_ref, lbf_ref, om_ref, tri_ref, yin_ref, g_ref, nw_ref,
         o_ref, st_ref, k_s, u_s) = refs
    else:
        (q_ref, f_ref, i_ref, lbf_ref, om_ref, tri_ref, o_ref, st_ref, k_s, u_s) = refs
    c = pl.program_id(1)
    n = HGRN_CHUNK

    @pl.when(c == 0)
    def _():
        st_ref[...] = jnp.zeros_like(st_ref)

    li = lax.broadcasted_iota(jnp.int32, (n, n), 0)
    si = lax.broadcasted_iota(jnp.int32, (n, n), 1)
    level_masks = []
    m = HGRN_DIAG
    while 2 * m <= n:
        sh = m.bit_length() - 1
        same_pair = (li >> (sh + 1)) == (si >> (sh + 1))
        q_half = ((li >> sh) & 1) == (0 if rev else 1)
        k_half = ((si >> sh) & 1) == (1 if rev else 0)
        level_masks.append((m, same_pair & q_half & k_half))
        m *= 2
    rowi = lax.broadcasted_iota(jnp.int32, (HGRN_DIAG, LANE), 0)
    lanei = lax.broadcasted_iota(jnp.int32, (HGRN_DIAG, LANE), 1)
    diag_cols = []
    for b in range(n // HGRN_DIAG):
        rel = lanei - b * HGRN_DIAG
        causal = (rowi <= rel) if rev else (rowi >= rel)
        diag_cols.append(jnp.where(causal, rel, -1))

    def chunk(j, carry):
        r0 = pl.multiple_of(((sub - 1 - j) if rev else j) * n, n)
        rows = pl.ds(r0, n)
        raw = f_ref[rows, :].astype(F32)
        t = jnp.exp(-jnp.abs(raw))
        r = 1.0 / (1.0 + t)
        nonneg = raw >= 0.0
        sig = jnp.where(nonneg, r, t * r)
        sig_neg = jnp.where(nonneg, t * r, r)
        om = om_ref[...]
        logf = jnp.log(lbf_ref[...] + om * sig)
        kk = om * sig_neg
        u = _split_dot(tri_ref[...], logf * LOG2E)
        qf = q_ref[rows, :].astype(F32) * (H_KEY ** -0.5)
        v = i_ref[rows, :].astype(F32)
        k_s[...] = kk
        u_s[...] = u

        u_end = u[0:1] if rev else u[n - 1:n]
        qb = (qf * jnp.exp2(u)).astype(BF16)
        kw = (kk * jnp.exp2(u_end - u)).astype(BF16)
        cd = jnp.exp2(u_end)

        levels = []
        for m, msk in level_masks:
            refs_rows = []
            for p in range(n // (2 * m)):
                ridx = 2 * m * p + (m if rev else m - 1)
                refs_rows.append(jnp.broadcast_to(u[ridx:ridx + 1], (2 * m, H_WIDTH)))
            rb = refs_rows[0] if len(refs_rows) == 1 else jnp.concatenate(refs_rows, axis=0)
            qt = (qf * jnp.exp2(u - rb)).astype(BF16)
            kt = (kk * jnp.exp2(rb - u)).astype(BF16)
            levels.append((qt, kt, msk))

        def diag_scores(h):
            sl = slice(h * LANE, (h + 1) * LANE)
            pieces = []
            for b in range(n // HGRN_DIAG):
                b0 = b * HGRN_DIAG
                q_blk = qf[b0:b0 + HGRN_DIAG, sl]
                u_blk = u[b0:b0 + HGRN_DIAG, sl]
                a_blk = jnp.zeros((HGRN_DIAG, LANE), F32)
                for s in range(HGRN_DIAG):
                    k_row = k_s[b0 + s:b0 + s + 1, sl]
                    u_row = u_s[b0 + s:b0 + s + 1, sl]
                    w = q_blk * (k_row * jnp.exp2(u_blk - u_row))
                    att = jnp.sum(w, axis=-1, keepdims=True)
                    a_blk = jnp.where(diag_cols[b] == s, att, a_blk)
                pieces.append(a_blk)
            return jnp.concatenate(pieces, axis=0)

        for h in range(H_HEADS):
            sl = slice(h * LANE, (h + 1) * LANE)
            att = diag_scores(h)
            for qt, kt, msk in levels:
                att = jnp.where(msk, _dot_nt(qt[:, sl], kt[:, sl]), att)
            st = st_ref[h]
            v_t = v[:, sl].T.astype(BF16)
            lhs = jnp.concatenate([att.astype(BF16), qb[:, sl]], axis=1)
            rhs = jnp.concatenate([v_t, st.astype(BF16)], axis=1)
            oh = _dot_nt(lhs, rhs)
            st_ref[h] = st * cd[:, sl] + _dot(v_t, kw[:, sl])
            if not rev:
                o_ref[rows, sl] = oh
            else:
                oh = oh + yin_ref[rows, sl]
                ms = jnp.mean(oh * oh, axis=-1, keepdims=True)
                gf = g_ref[rows, sl].astype(F32)
                o_ref[rows, sl] = (oh * lax.rsqrt(ms + EPS) * nw_ref[...]
                                   * (gf * _sigmoid(gf))).astype(o_ref.dtype)
        return carry

    lax.fori_loop(0, sub, chunk, 0, unroll=min(4, sub))


def _hgrn(rev, nb, seq, hin, lbf, om, tri, extra):
    sub = _chunks_per_step(seq // HGRN_CHUNK, CHUNKS_PER_STEP)
    n = sub * HGRN_CHUNK
    nc = seq // n
    t = nb * seq

    def blk(col):
        return lambda b, c: (b * nc + (nc - 1 - c if rev else c), col)

    const = lambda b, c: (0, 0)
    in_specs = [
        pl.BlockSpec((n, H_WIDTH), blk(0)),
        pl.BlockSpec((n, H_WIDTH), blk(2 if rev else 1)),
        pl.BlockSpec((n, H_WIDTH), blk(3)),
        pl.BlockSpec((1, H_WIDTH), const),
        pl.BlockSpec((1, H_WIDTH), const),
        pl.BlockSpec((HGRN_CHUNK, HGRN_CHUNK), const),
    ]
    args = [hin, hin, hin, lbf, om, tri]
    if rev:
        y_in, nw = extra
        in_specs += [pl.BlockSpec((n, H_WIDTH), blk(0)), pl.BlockSpec((n, H_WIDTH), blk(4)),
                     pl.BlockSpec((1, LANE), const)]
        args += [y_in, hin, nw]
        out_dtype = BF16
    else:
        out_dtype = F32
    return pl.pallas_call(
        functools.partial(_hgrn_kernel, rev, sub),
        grid=(nb, nc),
        in_specs=in_specs,
        out_specs=pl.BlockSpec((n, H_WIDTH), blk(0)),
        out_shape=jax.ShapeDtypeStruct((t, H_WIDTH), out_dtype),
        scratch_shapes=[pltpu.VMEM((H_HEADS, LANE, LANE), F32)]
        + [pltpu.VMEM((HGRN_CHUNK, H_WIDTH), F32) for _ in range(2)],
        compiler_params=_params(dimension_semantics=("arbitrary", "arbitrary")),
        name="hgrn_rev" if rev else "hgrn_fwd",
    )(*args)


def _attn_kernel(tq, tk, bounded_ref, q_ref, k_ref, vt_ref, o_ref, m_s, acc_s):
    nk = k_ref.shape[0] // tk
    for g in range(A_KV):
        gl = slice(g * LANE, (g + 1) * LANE)
        qs = jnp.concatenate(
            [q_ref[:, (g * A_GROUP + r) * LANE:(g * A_GROUP + r + 1) * LANE] for r in range(A_GROUP)], axis=0)
        acc_s[...] = jnp.zeros_like(acc_s)

        def tiles(kt):
            k0 = pl.multiple_of(kt * tk, tk)
            return k_ref[pl.ds(k0, tk), gl], vt_ref[gl, pl.ds(k0, tk)]

        def shifted_body(kt, carry):
            kb, vtb = tiles(kt)
            acc_s[...] += _dot(vtb, jnp.exp2(_dot_nt(kb, qs)).astype(BF16))
            return carry

        def online_body(kt, carry):
            kb, vtb = tiles(kt)
            s = _dot_nt(kb, qs)
            m_prev = m_s[...]
            m_new = jnp.maximum(m_prev, jnp.max(s, axis=0, keepdims=True))
            p = jnp.exp2(s - m_new)
            acc_s[...] = jnp.exp2(m_prev - m_new) * acc_s[...] + _dot(vtb, p.astype(BF16))
            m_s[...] = m_new
            return carry

        @pl.when(bounded_ref[0] != 0)
        def _():
            lax.fori_loop(0, nk, shifted_body, 0, unroll=min(4, nk))

        @pl.when(bounded_ref[0] == 0)
        def _():
            m_s[...] = jnp.full_like(m_s, -jnp.inf)
            lax.fori_loop(0, nk, online_body, 0)

        o = acc_s[...]
        o = o[:A_HD] / o[A_HD:A_HD + 1]
        for pr in range(A_GROUP // 2):
            pair = jnp.concatenate([o[:, (2 * pr) * tq:(2 * pr + 1) * tq],
                                    o[:, (2 * pr + 1) * tq:(2 * pr + 2) * tq]], axis=0)
            c0 = g * A_GROUP * A_HD + pr * LANE
            o_ref[:, c0:c0 + LANE] = pair.T.astype(o_ref.dtype)


def _attn(nb, seq, bounded, qh, kh, vt, tq, tk):
    t = nb * seq
    nq = seq // tq
    cols = A_GROUP * tq
    return pl.pallas_call(
        functools.partial(_attn_kernel, tq, tk),
        grid=(nb, nq),
        in_specs=[
            pl.BlockSpec(memory_space=pltpu.SMEM),
            pl.BlockSpec((tq, A_HEADS * LANE), lambda b, i: (b * nq + i, 0)),
            pl.BlockSpec((seq, A_KV * LANE), lambda b, i: (b, 0)),
            pl.BlockSpec((A_KV * LANE, seq), lambda b, i: (0, b)),
        ],
        out_specs=pl.BlockSpec((tq, A_HEADS * A_HD), lambda b, i: (b * nq + i, 0)),
        out_shape=jax.ShapeDtypeStruct((t, A_HEADS * A_HD), BF16),
        scratch_shapes=[pltpu.VMEM((1, cols), F32), pltpu.VMEM((LANE, cols), F32)],
        compiler_params=_params(dimension_semantics=("parallel", "parallel")),
        name="attn",
    )(bounded, qh, kh, vt)


def _merge_kernel(x_ref, ym_ref, yh_ref, ya_ref, g_ref, wb_ref, wo_ref, nw_ref, o_ref):
    mixed = None
    for i, y_ref in enumerate((ym_ref, yh_ref, ya_ref)):
        proj = _dot(y_ref[...], wb_ref[i])
        gate = _sigmoid(g_ref[:, i * D_MODEL:(i + 1) * D_MODEL].astype(F32))
        mixed = gate * proj if mixed is None else mixed + gate * proj
    out = _dot(mixed.astype(BF16), wo_ref[...])
    o_ref[...] = x_ref[...] + _rms(out, nw_ref[...])


def _merge(x, ym, yh, ya, gates, wb, wo, nw, tm):
    t = x.shape[0]
    row = lambda i: (i, 0)
    return pl.pallas_call(
        _merge_kernel,
        grid=(t // tm,),
        in_specs=[
            pl.BlockSpec((tm, D_MODEL), row),
            pl.BlockSpec((tm, M_INNER), row),
            pl.BlockSpec((tm, H_WIDTH), row),
            pl.BlockSpec((tm, A_HEADS * A_HD), row),
            pl.BlockSpec((tm, 3 * D_MODEL), row),
            pl.BlockSpec((3, M_INNER, D_MODEL), lambda i: (0, 0, 0)),
            pl.BlockSpec((D_MODEL, D_MODEL), lambda i: (0, 0)),
            pl.BlockSpec((1, D_MODEL), lambda i: (0, 0)),
        ],
        out_specs=pl.BlockSpec((tm, D_MODEL), row),
        out_shape=jax.ShapeDtypeStruct((t, D_MODEL), F32),
        compiler_params=_params(dimension_semantics=("parallel",)),
        name="merge",
    )(x, ym, yh, ya, gates, wb, wo, nw)


def _ffn_kernel(tiles_per_seq, xp_ref, xm_ref, xn_ref, nw1_ref, wup_ref, cw_ref, cb_ref, wdn_ref,
                nw2_ref, o_ref, act_s, *u_s):
    j = pl.program_id(0) % tiles_per_seq
    tm = xm_ref.shape[0]
    x = xm_ref[...]
    nw1 = nw1_ref[...]
    hp = jnp.where(j > 0, _rms(xp_ref[...], nw1), 0.0)
    hn = jnp.where(j < tiles_per_seq - 1, _rms(xn_ref[...], nw1), 0.0)
    hext = jnp.concatenate([hp, _rms(x, nw1), hn], axis=0).astype(BF16)

    def up(slot, c0):
        u_s[slot][...] = _dot(hext, wup_ref[:, c0:c0 + FFN_COLS])

    def conv(slot, c0):
        w = cw_ref[:, c0:c0 + FFN_COLS]
        out = cb_ref[:, c0:c0 + FFN_COLS]
        for k in range(3):
            out = out + u_s[slot][pl.ds(FFN_HALO - 1 + k, tm), :] * w[k:k + 1]
        return out

    n_chunks = D_FF // FFN_COLS
    acc = None
    group_start = 0
    up(0, 0)
    up(1, D_FF)
    for cc in range(n_chunks):
        c0 = cc * FFN_COLS
        par = 2 * (cc % 2)
        if cc + 1 < n_chunks:
            up(2 - par, c0 + FFN_COLS)
            up(3 - par, D_FF + c0 + FFN_COLS)
        ug = conv(par, c0)
        uu = conv(par + 1, D_FF + c0)
        act_s[:, c0:c0 + FFN_COLS] = (ug * _sigmoid(ug) * uu).astype(BF16)
        if (cc + 1) % FFN_DOWN_GROUP == 0 or cc + 1 == n_chunks:
            k0, k1 = group_start * FFN_COLS, c0 + FFN_COLS
            part = _dot(act_s[:, k0:k1], wdn_ref[k0:k1, :])
            acc = part if acc is None else acc + part
            group_start = cc + 1
    o_ref[...] = x + _rms(acc, nw2_ref[...])


def _ffn(x, nw1, wup, cw, cb, wdn, nw2, seq, tm):
    t = x.shape[0]
    tiles_per_seq = seq // tm
    hb = tm // FFN_HALO
    last_halo = t // FFN_HALO - 1
    const = lambda i: (0, 0)
    single = pl.Buffered(1)
    return pl.pallas_call(
        functools.partial(_ffn_kernel, tiles_per_seq),
        grid=(t // tm,),
        in_specs=[
            pl.BlockSpec((FFN_HALO, D_MODEL), lambda i: (jnp.maximum(i * hb - 1, 0), 0)),
            pl.BlockSpec((tm, D_MODEL), lambda i: (i, 0)),
            pl.BlockSpec((FFN_HALO, D_MODEL), lambda i: (jnp.minimum(i * hb + hb, last_halo), 0)),
            pl.BlockSpec((1, D_MODEL), const),
            pl.BlockSpec((D_MODEL, 2 * D_FF), const, pipeline_mode=single),
            pl.BlockSpec((3, 2 * D_FF), const),
            pl.BlockSpec((1, 2 * D_FF), const),
            pl.BlockSpec((D_FF, D_MODEL), const, pipeline_mode=single),
            pl.BlockSpec((1, D_MODEL), const),
        ],
        out_specs=pl.BlockSpec((tm, D_MODEL), lambda i: (i, 0)),
        out_shape=jax.ShapeDtypeStruct((t, D_MODEL), F32),
        scratch_shapes=[pltpu.VMEM((tm, D_FF), BF16)]
        + [pltpu.VMEM((tm + 2 * FFN_HALO, FFN_COLS), F32) for _ in range(4)],
        compiler_params=_params(dimension_semantics=("parallel",)),
        name="ffn",
    )(x, x, x, nw1, wup, cw, cb, wdn, nw2)


def _pack_w_in(w):
    sizes = (M_INNER, M_XBC, M_HEADS, M_HEADS, H_WIDTH, H_WIDTH, H_WIDTH, H_WIDTH, H_WIDTH,
             A_HEADS * A_HD, A_KV * A_HD, A_KV * A_HD, 3 * D_MODEL)
    parts, off = [], 0
    for s in sizes:
        parts.append(w[:, off:off + s])
        off += s
    (m_z, m_xbc, m_dtf, m_dtb, h_q, h_ff, h_fb, h_i, h_g, a_q, a_k, a_v, gate) = parts
    dt = jnp.pad(jnp.concatenate([m_dtf, m_dtb], axis=1), ((0, 0), (0, LANE - 2 * M_HEADS)))
    packed = jnp.concatenate(
        [m_z, m_xbc, dt, h_q, h_ff, h_fb, h_i, h_g, a_q, a_k, a_v, gate], axis=1)
    return packed.astype(BF16)


def _pad_lane(v):
    v = v.reshape(1, -1)
    return jnp.pad(v, ((0, 0), (0, LANE - v.shape[1])))


def _rope_tables(n):
    pos = jnp.arange(n)
    row = (pos // GRID_W).astype(F32)
    col = (pos % GRID_W).astype(F32)
    half = A_HD // 4
    inv = ROPE_BASE ** (-jnp.arange(0, 2 * half, 2, dtype=F32) / (2 * half))
    ar = row[:, None] * inv
    ac = col[:, None] * inv
    cr, sr, cc, sc = jnp.cos(ar), jnp.sin(ar), jnp.cos(ac), jnp.sin(ac)
    cos = jnp.concatenate([cr, cr, cc, cc] * (LANE // A_HD), axis=1)
    sin = jnp.concatenate([-sr, sr, -sc, sc] * (LANE // A_HD), axis=1)
    return cos, sin


def _tri(n, rev):
    i = jnp.arange(n)
    m = (i[None, :] >= i[:, None]) if rev else (i[None, :] <= i[:, None])
    return m.astype(BF16)


def _pick_tile(seq, want):
    return want if seq % want == 0 else seq


def kernel(x_prompt, x_sample, norm_mix_pre, w_in, m_conv_w, m_conv_b, m_dt_bias, m_a_log, m_d, m_norm_w, h_lb_logits, h_norm_w, a_q_norm, a_k_norm, w_branch, w_out, norm_mix_post, norm_ffn_pre, f_w_up, f_conv_w, f_conv_b, f_w_down, norm_ffn_post):
    sm = jax.nn.softmax(h_lb_logits.astype(F32), axis=0)
    tail = jnp.concatenate([jnp.zeros_like(sm[:1]), sm[1:]], axis=0)
    lower = jnp.cumsum(tail, axis=0)
    one_minus = sm[0:1] + (jnp.sum(tail, axis=0, keepdims=True) - lower)
    lower_floor = jnp.maximum(lower, LB_FLOOR)

    lane_a = jnp.arange(LANE) == A_HD
    kadd = lane_a.astype(F32).reshape(1, LANE)
    layers = []
    for l in range(DEPTH):
        score_bound = A_HD ** 0.5 * jnp.max(jnp.abs(a_q_norm[l])) * jnp.max(jnp.abs(a_k_norm[l]))
        bounded = score_bound <= ATT_SHIFT_LIMIT
        shift = jnp.where(bounded, score_bound * LOG2E, 0.0)
        layers.append(dict(
            bounded=bounded.astype(jnp.int32).reshape(1),
            qadd=jnp.where(lane_a, -shift, 0.0).astype(F32).reshape(1, LANE), kadd=kadd,
            nw_pre=norm_mix_pre[l].reshape(1, D_MODEL),
            w_in=_pack_w_in(w_in[l]),
            cw=m_conv_w[l], cb=m_conv_b[l].reshape(1, M_XBC),
            dtb=_pad_lane(m_dt_bias[l]), a_neg=_pad_lane(-jnp.exp(m_a_log[l].astype(F32)) * LOG2E),
            dskip=jnp.repeat(m_d[l], M_INNER // M_HEADS).reshape(1, M_INNER),
            m_nw=m_norm_w[l].reshape(1, M_INNER),
            lbf=lower_floor[l], om=one_minus[l],
            h_nw=h_norm_w[l].reshape(1, LANE),
            qw=jnp.tile(a_q_norm[l] * (A_HD ** -0.5 * LOG2E), LANE // A_HD).reshape(1, LANE),
            kw=jnp.tile(a_k_norm[l], LANE // A_HD).reshape(1, LANE),
            wb=w_branch[l].astype(BF16), wo=w_out[l].astype(BF16),
            nw_post=norm_mix_post[l].reshape(1, D_MODEL),
            nw_ffn=norm_ffn_pre[l].reshape(1, D_MODEL),
            wup=f_w_up[l].astype(BF16), fcw=f_conv_w[l], fcb=f_conv_b[l].reshape(1, 2 * D_FF),
            wdn=f_w_down[l].astype(BF16),
            nw_ffn_post=norm_ffn_post[l].reshape(1, D_MODEL),
        ))
    tri_ssd = (_tri(SSD_CHUNK, False), _tri(SSD_CHUNK, True))
    tri_hgrn = (_tri(HGRN_CHUNK, False), _tri(HGRN_CHUNK, True))
    head_of_lane = jnp.arange(M_INNER) // (M_INNER // M_HEADS)
    expand_ssd = tuple(
        (jnp.arange(LANE)[:, None] == d * M_HEADS + head_of_lane[None, :]).astype(BF16) for d in range(2))

    def run(x):
        nb, seq, _ = x.shape
        xf = x.reshape(nb * seq, D_MODEL)
        cos, sin = _rope_tables(seq)
        tm = _pick_tile(seq, 512)
        tq = _pick_tile(seq, 512)
        tk = _pick_tile(seq, 512)
        for p in layers:
            z, xbc, dtraw, hin, qh, kh, vh, gates = _inproj(
                xf, p["nw_pre"], p["w_in"], cos, sin, p["qw"], p["kw"],
                p["qadd"], p["kadd"], seq, tm)
            y_f, xc = _ssd_fwd(nb, seq, xbc, dtraw, p["cw"], p["cb"], p["dtb"], p["a_neg"],
                               tri_ssd[0], expand_ssd[0], p["dskip"])
            y_m = _ssd_rev(nb, seq, xc, dtraw, p["dtb"], p["a_neg"], tri_ssd[1], expand_ssd[1],
                           y_f, z, p["m_nw"])
            o_f = _hgrn(False, nb, seq, hin, p["lbf"][0:1], p["om"][0:1], tri_hgrn[0], None)
            y_h = _hgrn(True, nb, seq, hin, p["lbf"][1:2], p["om"][1:2], tri_hgrn[1], (o_f, p["h_nw"]))
            y_a = _attn(nb, seq, p["bounded"], qh, kh, vh, tq, tk)
            xf = _merge(xf, y_m, y_h, y_a, gates, p["wb"], p["wo"], p["nw_post"], tm)
            xf = _ffn(xf, p["nw_ffn"], p["wup"], p["fcw"], p["fcb"], p["wdn"], p["nw_ffn_post"], seq, tm)
        return xf.reshape(nb, seq, D_MODEL)

    return (run(x_prompt), run(x_sample))
```
